```python
import math
import jax, jax.numpy as jnp
from jax import lax
import numpy as np

D_MODEL = 1024
BATCH = 16
SEQ = 2048
DEPTH = 1

N_SB_HEADS = 8
SB_HEAD_DIM = 64
SB_WIDTH = N_SB_HEADS * SB_HEAD_DIM
SB_BLOCK = 128
SSM_WIDTH = D_MODEL // 2
SSM_GROUP = 16
SSM_GROUPS = SSM_WIDTH // SSM_GROUP
SSM_STATE = 64
D_FF = ((8 * D_MODEL // 3 + 127) // 128) * 128
CONV_WIDTH = 3
PLE_DIM = 256
DN_ALPHA = (2.0 * DEPTH) ** 0.25
DN_BETA = (8.0 * DEPTH) ** -0.25
LN_EPS = 1e-5
IN_WIDTH = 3 * SB_WIDTH + SSM_WIDTH + 2 * D_MODEL
_SPLITS = [SB_WIDTH, 2 * SB_WIDTH, 3 * SB_WIDTH, 3 * SB_WIDTH + SSM_WIDTH, 3 * SB_WIDTH + SSM_WIDTH + D_MODEL]

kernel_name = "hybrid_stickbreak_s5_convffn_deepnorm"


def _layer_norm(x, g, b):
    xf = x.astype(jnp.float32)
    mu = jnp.mean(xf, axis=-1, keepdims=True)
    var = jnp.mean(jnp.square(xf - mu), axis=-1, keepdims=True)
    return (xf - mu) * lax.rsqrt(var + LN_EPS) * g.astype(jnp.float32) + b.astype(jnp.float32)


def _stick_breaking_attention(q, k, v):
    b, t, _ = q.shape
    nb = t // SB_BLOCK
    def heads(a):
        return a.astype(jnp.float32).reshape(b, t, N_SB_HEADS, SB_HEAD_DIM).transpose(0, 2, 1, 3)
    qh, kh, vh = heads(q), heads(k), heads(v)
    scale = 1.0 / math.sqrt(SB_HEAD_DIM)
    q_blocks = qh.reshape(b, N_SB_HEADS, nb, SB_BLOCK, SB_HEAD_DIM).transpose(2, 0, 1, 3, 4)
    starts = jnp.arange(nb, dtype=jnp.int32) * SB_BLOCK
    kpos = jnp.arange(t, dtype=jnp.int32)

    def block(args):
        qb, start = args
        z = jnp.einsum('bhqd,bhkd->bhqk', qb, kh) * scale
        qpos = start + jnp.arange(SB_BLOCK, dtype=jnp.int32)
        mask = kpos[None, :] < qpos[:, None]
        log_om = jnp.where(mask, jax.nn.log_sigmoid(-z), 0.0)
        later = lax.cumsum(log_om, axis=3, reverse=True) - log_om
        w = jnp.where(mask, jnp.exp(jax.nn.log_sigmoid(z) + later), 0.0)
        return jnp.einsum('bhqk,bhkd->bhqd', w, vh)

    out = lax.map(block, (q_blocks, starts))
    return out.transpose(1, 0, 3, 2, 4).reshape(b, t, SB_WIDTH)


def _s5_scan(u, a_re, a_im, log_step, b_re, b_im, c_re, c_im, d):
    t = u.shape[1]
    f32 = jnp.float32
    lam = lax.complex(a_re.astype(f32), a_im.astype(f32))
    step = jnp.exp(log_step.astype(f32))[:, None]
    lam_bar = jnp.exp(lam * step)
    coef = (lam_bar - 1.0) / lam
    b_bar = coef[:, :, None] * lax.complex(b_re.astype(f32), b_im.astype(f32))
    uf = u.astype(f32)
    bu = lax.complex(jnp.einsum('btgh,gph->tbgp', uf, jnp.real(b_bar)),
                     jnp.einsum('btgh,gph->tbgp', uf, jnp.imag(b_bar)))
    a = jnp.broadcast_to(lam_bar[None, None], (t, 1) + lam_bar.shape)

    def combine(e1, e2):
        a1, b1 = e1
        a2, b2 = e2
        return a1 * a2, a2 * b1 + b2

    _, xs = lax.associative_scan(combine, (a, bu), axis=0)
    y = (jnp.einsum('tbgp,ghp->btgh', jnp.real(xs), c_re.astype(f32))
         - jnp.einsum('tbgp,ghp->btgh', jnp.imag(xs), c_im.astype(f32))
         + d.astype(f32) * uf)
    return y


def _causal_dwconv(x, w, bias):
    c = x.shape[-1]
    y = lax.conv_general_dilated(x, w[:, None, :].astype(x.dtype), window_strides=(1,),
                                 padding=((CONV_WIDTH - 1, 0),),
                                 dimension_numbers=('NWC', 'WIO', 'NWC'),
                                 feature_group_count=c)
    return y + bias.astype(x.dtype)


def setup_inputs(seed: int = 0) -> dict:
    key = jax.random.key(seed)
    ks = jax.random.split(key, 32)
    f32 = jnp.float32
    def nrm(k, shape, scale):
        return jax.random.normal(k, shape, f32) * scale
    L, G, P, H = DEPTH, SSM_GROUPS, SSM_STATE, SSM_GROUP
    a_im_base = jnp.pi * jnp.arange(P, dtype=f32)
    return {
        "x": nrm(ks[0], (BATCH, SEQ, D_MODEL), 1.0),
        "p": nrm(ks[1], (DEPTH, BATCH, SEQ, PLE_DIM), 1.0),
        "w_in": nrm(ks[2], (L, D_MODEL, IN_WIDTH), D_MODEL ** -0.5),
        "w_sb_out": nrm(ks[3], (L, SB_WIDTH, D_MODEL), SB_WIDTH ** -0.5),
        "ssm_a_re": -0.5 + nrm(ks[4], (L, G, P), 0.01),
        "ssm_a_im": a_im_base[None, None, :] + nrm(ks[5], (L, G, P), 0.01),
        "ssm_log_step": jax.random.uniform(ks[6], (L, G), f32, math.log(1e-3), math.log(1e-1)),
        "ssm_b_re": nrm(ks[7], (L, G, P, H), (2 * H) ** -0.5),
        "ssm_b_im": nrm(ks[8], (L, G, P, H), (2 * H) ** -0.5),
        "ssm_c_re": nrm(ks[9], (L, G, H, P), P ** -0.5),
        "ssm_c_im": nrm(ks[10], (L, G, H, P), P ** -0.5),
        "ssm_d": nrm(ks[11], (L, G, H), 1.0),
        "w_glu": nrm(ks[12], (L, SSM_WIDTH, 2 * D_MODEL), SSM_WIDTH ** -0.5),
        "w_o": nrm(ks[13], (L, D_MODEL, D_MODEL), DN_BETA * D_MODEL ** -0.5),
        "ln1_g": 1.0 + nrm(ks[14], (L, D_MODEL), 0.02),
        "ln1_b": nrm(ks[15], (L, D_MODEL), 0.02),
        "w_up": nrm(ks[16], (L, D_MODEL, 2 * D_FF), D_MODEL ** -0.5),
        "conv_w": nrm(ks[17], (L, CONV_WIDTH, 2 * D_FF), CONV_WIDTH ** -0.5),
        "conv_b": nrm(ks[18], (L, 2 * D_FF), 0.02),
        "w_down": nrm(ks[19], (L, D_FF, D_MODEL), DN_BETA * D_FF ** -0.5),
        "w_pe": nrm(ks[20], (L, PLE_DIM, D_MODEL), DN_BETA * PLE_DIM ** -0.5),
        "w_pe_gate": nrm(ks[21], (L, D_MODEL, D_MODEL), D_MODEL ** -0.5),
        "ln2_g": 1.0 + nrm(ks[22], (L, D_MODEL), 0.02),
        "ln2_b": nrm(ks[23], (L, D_MODEL), 0.02),
    }


def reference(x, p, w_in, w_sb_out, ssm_a_re, ssm_a_im, ssm_log_step, ssm_b_re, ssm_b_im,
              ssm_c_re, ssm_c_im, ssm_d, w_glu, w_o, ln1_g, ln1_b, w_up, conv_w, conv_b,
              w_down, w_pe, w_pe_gate, ln2_g, ln2_b):
    dt = x.dtype
    b, t, _ = x.shape
    h = x
    for i in range(DEPTH):
        proj = h @ w_in[i]
        q, k, v, u, g_attn, g_ssm = jnp.split(proj, _SPLITS, axis=-1)
        attn = _stick_breaking_attention(q, k, v).astype(dt)
        attn_br = attn @ w_sb_out[i]
        y = _s5_scan(u.reshape(b, t, SSM_GROUPS, SSM_GROUP), ssm_a_re[i], ssm_a_im[i],
                     ssm_log_step[i], ssm_b_re[i], ssm_b_im[i], ssm_c_re[i], ssm_c_im[i], ssm_d[i])
        y = jax.nn.gelu(y).reshape(b, t, SSM_WIDTH).astype(dt)
        z = y @ w_glu[i]
        ssm_br = z[..., :D_MODEL] * jax.nn.sigmoid(z[..., D_MODEL:])
        mixed = jax.nn.sigmoid(g_attn) * attn_br + jax.nn.sigmoid(g_ssm) * ssm_br
        h = _layer_norm(DN_ALPHA * h + mixed @ w_o[i], ln1_g[i], ln1_b[i]).astype(dt)
        up = _causal_dwconv(h @ w_up[i], conv_w[i], conv_b[i])
        val, gate = up[..., :D_FF], up[..., D_FF:]
        ffn = (jax.nn.silu(gate) * val) @ w_down[i]
        ple = (p[i].astype(dt) @ w_pe[i]) * jax.nn.sigmoid(h @ w_pe_gate[i])
        h = _layer_norm(DN_ALPHA * h + ffn + ple, ln2_g[i], ln2_b[i]).astype(dt)
    return h
```

```python
import functools
import math

import jax
import jax.numpy as jnp
from jax import lax
from jax.experimental import pallas as pl
from jax.experimental.pallas import tpu as pltpu

F32 = jnp.float32
BF16 = jnp.bfloat16

N_SB_HEADS = 8
SB_HEAD_DIM = 64
CONV_WIDTH = 3
DEPTH = 1
DN_ALPHA = (2.0 * DEPTH) ** 0.25
LN_EPS = 1e-5

LANES = 128
MXU_DIM = 256
BF16_SUBLANES = 16

VMEM_LIMIT = 56 * 1024 * 1024


def _dot(a, b):
    return jnp.dot(a, b, preferred_element_type=F32)


def _dot_nt(a, b):
    return lax.dot_general(a, b, (((1,), (1,)), ((), ())), preferred_element_type=F32)


def _layer_norm(h, g, b):
    mu = jnp.mean(h, axis=-1, keepdims=True)
    d = h - mu
    var = jnp.mean(d * d, axis=-1, keepdims=True)
    return d * lax.rsqrt(var + LN_EPS) * g + b


def _resident(shape):
    nd = len(shape)
    return pl.BlockSpec(shape, lambda *_: (0,) * nd, pipeline_mode=pl.Buffered(1))


def _ssm_prep_kernel(are_ref, aim_ref, ls_ref, bre_ref, bim_ref,
                     lre_ref, lim_ref, bbre_ref, bbim_ref):
    a_re, a_im = are_ref[...], aim_ref[...]
    step = jnp.exp(ls_ref[...])
    mag = jnp.exp(a_re * step)
    l_re = mag * jnp.cos(a_im * step)
    l_im = mag * jnp.sin(a_im * step)
    n_re, n_im = l_re - 1.0, l_im
    den = a_re * a_re + a_im * a_im
    c_re = (n_re * a_re + n_im * a_im) / den
    c_im = (n_im * a_re - n_re * a_im) / den
    b_re, b_im = bre_ref[...], bim_ref[...]
    lre_ref[...] = l_re
    lim_ref[...] = l_im
    bbre_ref[...] = c_re * b_re - c_im * b_im
    bbim_ref[...] = c_re * b_im + c_im * b_re


def _ssm_prep(a_re, a_im, log_step, b_re, b_im):
    g, p = a_re.shape
    h = b_re.shape[-1]
    rows = g * p
    col = lambda a: a.reshape(rows, 1).astype(F32)
    ls = jnp.broadcast_to(log_step[:, None], (g, p))
    out = pl.pallas_call(
        _ssm_prep_kernel,
        out_shape=[jax.ShapeDtypeStruct((rows, 1), F32)] * 2
        + [jax.ShapeDtypeStruct((rows, h), F32)] * 2,
        name="ssm_prep",
    )(col(a_re), col(a_im), col(ls), b_re.reshape(rows, h).astype(F32),
      b_im.reshape(rows, h).astype(F32))
    l_re, l_im, bb_re, bb_im = out
    return (l_re.reshape(g, p), l_im.reshape(g, p),
            bb_re.reshape(g, p, h), bb_im.reshape(g, p, h))


def _proj_kernel(x_ref, w_ref, q_ref, k_ref, v_ref, u_ref, *, sbw, scale):
    xb = x_ref[...].astype(BF16)
    q_ref[...] = (_dot(xb, w_ref[:, 0:sbw]) * scale).astype(BF16)
    k_ref[...] = _dot(xb, w_ref[:, sbw:2 * sbw]).astype(BF16)
    v_ref[...] = _dot(xb, w_ref[:, 2 * sbw:3 * sbw]).astype(BF16)
    u_ref[...] = _dot(xb, w_ref[:, 3 * sbw:])


def _proj(x2, w_qkvu, b, t, tm, sbw, ssw):
    n, d = x2.shape
    tpb = t // tm
    row = lambda i: (i, 0)
    return pl.pallas_call(
        functools.partial(_proj_kernel, sbw=sbw, scale=1.0 / math.sqrt(SB_HEAD_DIM)),
        grid=(n // tm,),
        in_specs=[pl.BlockSpec((tm, d), row), _resident(w_qkvu.shape)],
        out_specs=[pl.BlockSpec((tm, sbw), row)] * 3
        + [pl.BlockSpec((tm, ssw), lambda i: (i % tpb, i // tpb))],
        out_shape=[jax.ShapeDtypeStruct((n, sbw), BF16)] * 3
        + [jax.ShapeDtypeStruct((t, b * ssw), F32)],
        compiler_params=pltpu.CompilerParams(
            dimension_semantics=("parallel",), vmem_limit_bytes=VMEM_LIMIT),
        name="proj",
    )(x2, w_qkvu)


def _attn_kernel(q_ref, k_ref, v_ref, tri_ref, o_ref, *, tq):
    qi = pl.program_id(2)
    tri = tri_ref[...]
    rows = lax.broadcasted_iota(jnp.int32, (tq, tq), 0)
    cols = lax.broadcasted_iota(jnp.int32, (tq, tq), 1)
    causal = cols < rows
    hd = SB_HEAD_DIM

    def scores(q, kb, mask):
        z = _dot_nt(q, kb)
        sp = jnp.maximum(z, 0.0) + jnp.log(1.0 + jnp.exp(-jnp.abs(z)))
        lom = -sp
        if mask is not None:
            lom = jnp.where(mask, lom, 0.0)
        hi = lom.astype(BF16)
        lo = (lom - hi.astype(F32)).astype(BF16)
        suffix = _dot(hi, tri) + _dot(lo, tri)
        return z - sp, lom, suffix

    outs = []
    for h in range(LANES // hd):
        sl = slice(h * hd, (h + 1) * hd)
        q = q_ref[0, :, sl]
        d0 = pl.multiple_of(qi * tq, tq)
        ls, lom, suffix = scores(q, k_ref[0, pl.ds(d0, tq), sl], causal)
        w = jnp.where(causal, jnp.exp(ls + suffix), 0.0)
        acc0 = _dot(w.astype(BF16), v_ref[0, pl.ds(d0, tq), sl])
        c0 = jnp.sum(lom, axis=-1, keepdims=True)

        def body(i, carry, q=q, sl=sl):
            acc, c = carry
            j0 = pl.multiple_of((qi - 1 - i) * tq, tq)
            ls, lom, suffix = scores(q, k_ref[0, pl.ds(j0, tq), sl], None)
            w = jnp.exp(ls + suffix + c)
            acc = acc + _dot(w.astype(BF16), v_ref[0, pl.ds(j0, tq), sl])
            return acc, c + jnp.sum(lom, axis=-1, keepdims=True)

        acc, _ = lax.fori_loop(0, qi, body, (acc0, c0))
        outs.append(acc)
    o_ref[0] = jnp.concatenate(outs, axis=-1).astype(BF16)


def _attention(q, k, v, tq):
    b, t, w = q.shape
    tri = (jnp.arange(tq)[:, None] > jnp.arange(tq)[None, :]).astype(BF16)
    qspec = pl.BlockSpec((1, tq, LANES), lambda bi, hp, qi: (bi, qi, hp))
    kvspec = pl.BlockSpec((1, t, LANES), lambda bi, hp, qi: (bi, 0, hp))
    return pl.pallas_call(
        functools.partial(_attn_kernel, tq=tq),
        grid=(b, w // LANES, t // tq),
        in_specs=[qspec, kvspec, kvspec, pl.BlockSpec((tq, tq), lambda bi, hp, qi: (0, 0))],
        out_specs=qspec,
        out_shape=jax.ShapeDtypeStruct((b, t, w), BF16),
        compiler_params=pltpu.CompilerParams(
            dimension_semantics=("parallel", "parallel", "arbitrary"),
            vmem_limit_bytes=VMEM_LIMIT),
        name="attn",
    )(q, k, v, tri)


def _ssm_kernel(u_ref, bt_ref, cre_ref, cim_ref, lre_ref, lim_ref, d_ref, y_ref,
                s_ref, st_ref, *, tt, nb, ns, strip):
    @pl.when(pl.program_id(0) == 0)
    def _():
        st_ref[...] = jnp.zeros_like(st_ref)

    u = u_ref[...]
    ub = u.astype(BF16)
    n_tiles = 2 * ns // MXU_DIM
    for n in range(n_tiles):
        kt = (n % (n_tiles // 2)) * MXU_DIM // (4 * LANES)
        s_ref[:, n * MXU_DIM:(n + 1) * MXU_DIM] = _dot(
            ub[:, kt * LANES:(kt + 1) * LANES], bt_ref[n])

    for s in range(ns // strip):
        re = slice(s * strip, (s + 1) * strip)
        im = slice(ns + s * strip, ns + (s + 1) * strip)
        lr = jnp.broadcast_to(lre_ref[:, re], (nb, strip))
        li = jnp.broadcast_to(lim_ref[:, re], (nb, strip))

        def step(ti, carry, re=re, im=im, lr=lr, li=li):
            xr, xi = carry
            r0 = pl.multiple_of(ti * nb, nb)
            nr = lr * xr - li * xi + s_ref[pl.ds(r0, nb), re]
            ni = lr * xi + li * xr + s_ref[pl.ds(r0, nb), im]
            s_ref[pl.ds(r0, nb), re] = nr
            s_ref[pl.ds(r0, nb), im] = ni
            return nr, ni

        xr, xi = lax.fori_loop(0, tt, step, (st_ref[:, re], st_ref[:, im]), unroll=4)
        st_ref[:, re] = xr
        st_ref[:, im] = xi

    width = u.shape[1]
    kw = ns * MXU_DIM // width
    for n in range(width // MXU_DIM):
        oc = slice(n * MXU_DIM, (n + 1) * MXU_DIM)
        xre = s_ref[:, n * kw:(n + 1) * kw].astype(BF16)
        xim = s_ref[:, ns + n * kw:ns + (n + 1) * kw].astype(BF16)
        y = _dot(xre, cre_ref[n]) - _dot(xim, cim_ref[n]) + d_ref[:, oc] * u[:, oc]
        y_ref[:, oc] = jax.nn.gelu(y).astype(BF16)


def _ssm(u2, bt, cre, cim, lre, lim, d_row, nb, tt):
    rows, width = u2.shape
    ns = lre.shape[1]
    blk = tt * nb
    return pl.pallas_call(
        functools.partial(_ssm_kernel, tt=tt, nb=nb, ns=ns, strip=4 * LANES),
        grid=(rows // blk,),
        in_specs=[pl.BlockSpec((blk, width), lambda i: (i, 0)),
                  _resident(bt.shape), _resident(cre.shape), _resident(cim.shape),
                  _resident(lre.shape), _resident(lim.shape), _resident(d_row.shape)],
        out_specs=pl.BlockSpec((blk, width), lambda i: (i, 0)),
        out_shape=jax.ShapeDtypeStruct((rows, width), BF16),
        scratch_shapes=[pltpu.VMEM((blk, 2 * ns), F32), pltpu.VMEM((nb, 2 * ns), F32)],
        compiler_params=pltpu.CompilerParams(
            dimension_semantics=("arbitrary",), vmem_limit_bytes=VMEM_LIMIT),
        name="ssm",
    )(u2, bt, cre, cim, lre, lim, d_row)


def _ssm_weights(l_re, l_im, bb_re, bb_im, c_re, c_im):
    g, p, h = bb_re.shape
    ns = g * p
    eye = jnp.eye(g, dtype=F32)
    b_full = jnp.concatenate(
        [jnp.einsum('gph,gk->ghkp', bb, eye).reshape(g * h, ns) for bb in (bb_re, bb_im)],
        axis=1).astype(BF16)
    n_tiles = 2 * ns // MXU_DIM
    tiles = []
    for n in range(n_tiles):
        kt = (n % (n_tiles // 2)) * MXU_DIM // (4 * LANES)
        tiles.append(b_full[kt * LANES:(kt + 1) * LANES, n * MXU_DIM:(n + 1) * MXU_DIM])
    bt = jnp.stack(tiles)
    width = g * h
    kw = ns * MXU_DIM // width

    def ctiles(c):
        full = jnp.einsum('ghp,gk->gpkh', c.astype(F32), eye).reshape(ns, width).astype(BF16)
        return jnp.stack([full[n * kw:(n + 1) * kw, n * MXU_DIM:(n + 1) * MXU_DIM]
                          for n in range(width // MXU_DIM)])

    return bt, ctiles(c_re), ctiles(c_im), l_re.reshape(1, ns), l_im.reshape(1, ns)


def _merge_kernel(x_ref, a_ref, y_ref, wg_ref, wsb_ref, wglu_ref, wo_ref, g_ref, b_ref,
                  o_ref, *, cw):
    x = x_ref[...]
    xb = x.astype(BF16)
    attn = a_ref[...]
    y = y_ref[...]
    d = x.shape[1]
    acc = DN_ALPHA * x
    for c in range(d // cw):
        cs = slice(c * cw, (c + 1) * cw)
        gs = slice(d + c * cw, d + (c + 1) * cw)
        attn_br = _dot(attn, wsb_ref[:, cs])
        ssm_br = _dot(y, wglu_ref[:, cs]) * jax.nn.sigmoid(_dot(y, wglu_ref[:, gs]))
        mixed = (jax.nn.sigmoid(_dot(xb, wg_ref[:, cs])) * attn_br
                 + jax.nn.sigmoid(_dot(xb, wg_ref[:, gs])) * ssm_br)
        acc = acc + _dot(mixed.astype(BF16), wo_ref[cs, :])
    o_ref[...] = _layer_norm(acc, g_ref[...], b_ref[...])


def _merge(x2, attn2, y_tm, w_g, w_sb, w_glu, w_o, ln_g, ln_b, t, tm):
    n, d = x2.shape
    sbw = attn2.shape[1]
    ssw = w_glu.shape[0]
    tpb = t // tm
    row = lambda i: (i, 0)
    return pl.pallas_call(
        functools.partial(_merge_kernel, cw=2 * MXU_DIM),
        grid=(n // tm,),
        in_specs=[pl.BlockSpec((tm, d), row), pl.BlockSpec((tm, sbw), row),
                  pl.BlockSpec((tm, ssw), lambda i: (i % tpb, i // tpb)),
                  _resident(w_g.shape), _resident(w_sb.shape), _resident(w_glu.shape),
                  _resident(w_o.shape), _resident(ln_g.shape), _resident(ln_b.shape)],
        out_specs=pl.BlockSpec((tm, d), row),
        out_shape=jax.ShapeDtypeStruct((n, d), F32),
        compiler_params=pltpu.CompilerParams(
            dimension_semantics=("parallel",), vmem_limit_bytes=VMEM_LIMIT),
        name="merge",
    )(x2, attn2, y_tm, w_g, w_sb, w_glu, w_o, ln_g, ln_b)


def _ffn_kernel(h_ref, halo_ref, p_ref, wup_ref, cw_ref, cb_ref, wdn_ref, wpe_ref, wpg_ref,
                g_ref, b_ref, o_ref, acc_ref, *, tpb, n_chunks, halo):
    h = h_ref[...]
    hb = h.astype(BF16)
    first = pl.program_id(0) % tpb == 0
    hb_ext = jnp.concatenate(
        [jnp.where(first, 0.0, halo_ref[...]).astype(BF16), hb], axis=0)

    def conv_up(j):
        up = _dot(hb_ext, wup_ref[j])
        w = cw_ref[j]
        out = (w[2:3] * up + w[1:2] * pltpu.roll(up, 1, 0) + w[0:1] * pltpu.roll(up, 2, 0)
               + cb_ref[j])
        return out[halo:]

    acc_ref[...] = DN_ALPHA * h + _dot(p_ref[...].astype(BF16), wpe_ref[...]) * jax.nn.sigmoid(
        _dot(hb, wpg_ref[...]))

    def chunk(j, carry):
        val = conv_up(j)
        gate = conv_up(n_chunks + j)
        act = (gate * jax.nn.sigmoid(gate) * val).astype(BF16)
        acc_ref[...] += _dot(act, wdn_ref[j])
        return carry

    lax.fori_loop(0, n_chunks, chunk, 0)
    o_ref[...] = _layer_norm(acc_ref[...], g_ref[...], b_ref[...])


def _ffn(h1, p2, w_up_t, conv_w_t, conv_b_t, w_dn_t, w_pe, w_pg, ln_g, ln_b, t, tm):
    n, d = h1.shape
    pd = p2.shape[1]
    halo = BF16_SUBLANES
    tpb = t // tm
    n_chunks = w_dn_t.shape[0]
    row = lambda i: (i, 0)
    return pl.pallas_call(
        functools.partial(_ffn_kernel, tpb=tpb, n_chunks=n_chunks, halo=halo),
        grid=(n // tm,),
        in_specs=[pl.BlockSpec((tm, d), row),
                  pl.BlockSpec((halo, d), lambda i: (jnp.maximum(i * (tm // halo) - 1, 0), 0)),
                  pl.BlockSpec((tm, pd), row),
                  _resident(w_up_t.shape), _resident(conv_w_t.shape), _resident(conv_b_t.shape),
                  _resident(w_dn_t.shape), _resident(w_pe.shape), _resident(w_pg.shape),
                  _resident(ln_g.shape), _resident(ln_b.shape)],
        out_specs=pl.BlockSpec((tm, d), row),
        out_shape=jax.ShapeDtypeStruct((n, d), F32),
        scratch_shapes=[pltpu.VMEM((tm, d), F32)],
        compiler_params=pltpu.CompilerParams(
            dimension_semantics=("parallel",), vmem_limit_bytes=VMEM_LIMIT),
        name="ffn",
    )(h1, h1, p2, w_up_t, conv_w_t, conv_b_t, w_dn_t, w_pe, w_pg, ln_g, ln_b)


def kernel(x, p, w_in, w_sb_out, ssm_a_re, ssm_a_im, ssm_log_step, ssm_b_re, ssm_b_im,
           ssm_c_re, ssm_c_im, ssm_d, w_glu, w_o, ln1_g, ln1_b, w_up, conv_w, conv_b,
           w_down, w_pe, w_pe_gate, ln2_g, ln2_b):
    b, t, d = x.shape
    assert w_in.shape[0] == DEPTH
    sbw = N_SB_HEADS * SB_HEAD_DIM
    ssw = w_glu.shape[1]
    dff = w_down.shape[1]
    n = b * t
    tm = min(512, t)
    tq = min(MXU_DIM, t)
    tt = min(32, t)
    assert t % tm == 0 and t % tq == 0 and t % tt == 0 and dff % MXU_DIM == 0
    n_chunks = dff // MXU_DIM

    h = x.reshape(n, d)
    for i in range(DEPTH):
        wi = w_in[i].astype(BF16)
        w_qkvu, w_gates = wi[:, :3 * sbw + ssw], wi[:, 3 * sbw + ssw:]

        q, k, v, u_tm = _proj(h, w_qkvu, b, t, tm, sbw, ssw)
        attn = _attention(q.reshape(b, t, sbw), k.reshape(b, t, sbw), v.reshape(b, t, sbw), tq)

        l_re, l_im, bb_re, bb_im = _ssm_prep(ssm_a_re[i], ssm_a_im[i], ssm_log_step[i],
                                             ssm_b_re[i], ssm_b_im[i])
        bt, cre, cim, lre, lim = _ssm_weights(l_re, l_im, bb_re, bb_im, ssm_c_re[i], ssm_c_im[i])
        y2 = _ssm(u_tm.reshape(t * b, ssw), bt, cre, cim, lre, lim,
                  ssm_d[i].reshape(1, ssw).astype(F32), b, tt)

        h = _merge(h, attn.reshape(n, sbw), y2.reshape(t, b * ssw), w_gates,
                   w_sb_out[i].astype(BF16), w_glu[i].astype(BF16), w_o[i].astype(BF16),
                   ln1_g[i].reshape(1, d), ln1_b[i].reshape(1, d), t, tm)

        w_up_t = w_up[i].astype(BF16).reshape(d, 2 * n_chunks, MXU_DIM).transpose(1, 0, 2)
        conv_w_t = conv_w[i].reshape(CONV_WIDTH, 2 * n_chunks, MXU_DIM).transpose(1, 0, 2)
        conv_b_t = conv_b[i].reshape(2 * n_chunks, 1, MXU_DIM)
        w_dn_t = w_down[i].astype(BF16).reshape(n_chunks, MXU_DIM, d)
        h = _ffn(h, p[i].reshape(n, -1), w_up_t, conv_w_t, conv_b_t, w_dn_t,
                 w_pe[i].astype(BF16), w_pe_gate[i].astype(BF16),
                 ln2_g[i].reshape(1, d), ln2_b[i].reshape(1, d), t, tm)
    return h.reshape(b, t, d)
```

```python
import functools
import math

import jax
import jax.numpy as jnp
from jax import lax
from jax.experimental import pallas as pl
from jax.experimental.pallas import tpu as pltpu

F32 = jnp.float32
BF16 = jnp.bfloat16

N_SB_HEADS = 8
SB_HEAD_DIM = 64
CONV_WIDTH = 3
DEPTH = 1
DN_ALPHA = (2.0 * DEPTH) ** 0.25
LN_EPS = 1e-5

LANES = 128
MXU_DIM = 256
BF16_SUBLANES = 16

VMEM_LIMIT = 56 * 1024 * 1024
ATTN_HEADS_PER_STEP = 4


def _dot(a, b):
    return jnp.dot(a, b, preferred_element_type=F32)


def _dot_nt(a, b):
    return lax.dot_general(a, b, (((1,), (1,)), ((), ())), preferred_element_type=F32)


def _layer_norm(h, g, b):
    mu = jnp.mean(h, axis=-1, keepdims=True)
    d = h - mu
    var = jnp.mean(d * d, axis=-1, keepdims=True)
    return d * lax.rsqrt(var + LN_EPS) * g + b


def _resident(shape):
    nd = len(shape)
    return pl.BlockSpec(shape, lambda *_: (0,) * nd, pipeline_mode=pl.Buffered(1))


def _ssm_prep_kernel(are_ref, aim_ref, ls_ref, bre_ref, bim_ref,
                     lre_ref, lim_ref, bbre_ref, bbim_ref):
    a_re, a_im = are_ref[...], aim_ref[...]
    step = jnp.exp(ls_ref[...])
    mag = jnp.exp(a_re * step)
    l_re = mag * jnp.cos(a_im * step)
    l_im = mag * jnp.sin(a_im * step)
    n_re, n_im = l_re - 1.0, l_im
    den = a_re * a_re + a_im * a_im
    c_re = (n_re * a_re + n_im * a_im) / den
    c_im = (n_im * a_re - n_re * a_im) / den
    b_re, b_im = bre_ref[...], bim_ref[...]
    lre_ref[...] = l_re
    lim_ref[...] = l_im
    bbre_ref[...] = c_re * b_re - c_im * b_im
    bbim_ref[...] = c_re * b_im + c_im * b_re


def _ssm_prep(a_re, a_im, log_step, b_re, b_im):
    g, p = a_re.shape
    h = b_re.shape[-1]
    rows = g * p
    col = lambda a: a.reshape(rows, 1).astype(F32)
    ls = jnp.broadcast_to(log_step[:, None], (g, p))
    out = pl.pallas_call(
        _ssm_prep_kernel,
        out_shape=[jax.ShapeDtypeStruct((rows, 1), F32)] * 2
        + [jax.ShapeDtypeStruct((rows, h), F32)] * 2,
        name="ssm_prep",
    )(col(a_re), col(a_im), col(ls), b_re.reshape(rows, h).astype(F32),
      b_im.reshape(rows, h).astype(F32))
    l_re, l_im, bb_re, bb_im = out
    return (l_re.reshape(g, p), l_im.reshape(g, p),
            bb_re.reshape(g, p, h), bb_im.reshape(g, p, h))


def _proj_kernel(x_ref, w_ref, q_ref, k_ref, v_ref, u_ref, *, sbw, scale):
    xb = x_ref[...].astype(BF16)
    q_ref[...] = (_dot(xb, w_ref[:, 0:sbw]) * scale).astype(BF16)
    k_ref[...] = _dot(xb, w_ref[:, sbw:2 * sbw]).astype(BF16)
    v_ref[...] = _dot(xb, w_ref[:, 2 * sbw:3 * sbw]).astype(BF16)
    u_ref[...] = _dot(xb, w_ref[:, 3 * sbw:])


def _proj(x2, w_qkvu, b, t, tm, sbw, ssw):
    n, d = x2.shape
    tpb = t // tm
    row = lambda i: (i, 0)
    return pl.pallas_call(
        functools.partial(_proj_kernel, sbw=sbw, scale=1.0 / math.sqrt(SB_HEAD_DIM)),
        grid=(n // tm,),
        in_specs=[pl.BlockSpec((tm, d), row), _resident(w_qkvu.shape)],
        out_specs=[pl.BlockSpec((tm, sbw), row)] * 3
        + [pl.BlockSpec((tm, ssw), lambda i: (i % tpb, i // tpb))],
        out_shape=[jax.ShapeDtypeStruct((n, sbw), BF16)] * 3
        + [jax.ShapeDtypeStruct((t, b * ssw), F32)],
        compiler_params=pltpu.CompilerParams(
            dimension_semantics=("parallel",), vmem_limit_bytes=VMEM_LIMIT),
        name="proj",
    )(x2, w_qkvu)


def _attn_kernel(q_ref, k_ref, v_ref, tri_ref, o_ref, *, tq, heads):
    qi = pl.program_id(2)
    neg_tri = tri_ref[...]
    rows = lax.broadcasted_iota(jnp.int32, (tq, tq), 0)
    cols = lax.broadcasted_iota(jnp.int32, (tq, tq), 1)
    causal = cols < rows
    hd = SB_HEAD_DIM

    def tiles(k0, cs, mask):
        sls = [slice(h * hd, (h + 1) * hd) for h in range(heads)]
        zs = [_dot_nt(q_ref[0, :, sl], k_ref[0, pl.ds(k0, tq), sl]) for sl in sls]
        sps = []
        for z in zs:
            sp = jnp.maximum(z, 0.0) + jnp.log(1.0 + jnp.exp(-jnp.abs(z)))
            sps.append(sp if mask is None else jnp.where(mask, sp, 0.0))
        args = []
        for z, sp, c in zip(zs, sps, cs):
            arg = (z - sp) + _dot(sp.astype(BF16), neg_tri)
            args.append(arg if c is None else arg - c)
        pvs = []
        for arg, sl in zip(args, sls):
            w = jnp.exp(arg)
            if mask is not None:
                w = jnp.where(mask, w, 0.0)
            pvs.append(_dot(w.astype(BF16), v_ref[0, pl.ds(k0, tq), sl]))
        return pvs, [jnp.sum(sp, axis=-1, keepdims=True) for sp in sps]

    d0 = pl.multiple_of(qi * tq, tq)
    init = tuple(zip(*tiles(d0, [None] * heads, causal)))

    def body(i, carry):
        k0 = pl.multiple_of((qi - 1 - i) * tq, tq)
        pvs, sums = tiles(k0, [c for _, c in carry], None)
        return tuple((acc + pv, c + s) for (acc, c), pv, s in zip(carry, pvs, sums))

    final = lax.fori_loop(0, qi, body, init)
    o_ref[0] = jnp.concatenate([acc for acc, _ in final], axis=-1).astype(BF16)


def _attention(q, k, v, tq, heads):
    b, t, w = q.shape
    lanes = heads * SB_HEAD_DIM
    tri = -(jnp.arange(tq)[:, None] > jnp.arange(tq)[None, :]).astype(BF16)
    qspec = pl.BlockSpec((1, tq, lanes), lambda bi, hp, qi: (bi, qi, hp))
    kvspec = pl.BlockSpec((1, t, lanes), lambda bi, hp, qi: (bi, 0, hp))
    return pl.pallas_call(
        functools.partial(_attn_kernel, tq=tq, heads=heads),
        grid=(b, w // lanes, t // tq),
        in_specs=[qspec, kvspec, kvspec, pl.BlockSpec((tq, tq), lambda bi, hp, qi: (0, 0))],
        out_specs=qspec,
        out_shape=jax.ShapeDtypeStruct((b, t, w), BF16),
        compiler_params=pltpu.CompilerParams(
            dimension_semantics=("parallel", "parallel", "arbitrary"),
            vmem_limit_bytes=VMEM_LIMIT),
        name="attn",
    )(q, k, v, tri)


def _ssm_kernel(u_ref, bt_ref, cre_ref, cim_ref, lre_ref, lim_ref, d_ref, y_ref,
                s_ref, st_ref, *, tt, nb, ns, strip):
    @pl.when(pl.program_id(0) == 0)
    def _():
        st_ref[...] = jnp.zeros_like(st_ref)

    u = u_ref[...]
    ub = u.astype(BF16)
    n_tiles = 2 * ns // MXU_DIM
    for n in range(n_tiles):
        kt = (n % (n_tiles // 2)) * MXU_DIM // (4 * LANES)
        s_ref[:, n * MXU_DIM:(n + 1) * MXU_DIM] = _dot(
            ub[:, kt * LANES:(kt + 1) * LANES], bt_ref[n])

    for s in range(ns // strip):
        re = slice(s * strip, (s + 1) * strip)
        im = slice(ns + s * strip, ns + (s + 1) * strip)
        lr = jnp.broadcast_to(lre_ref[:, re], (nb, strip))
        li = jnp.broadcast_to(lim_ref[:, re], (nb, strip))

        def step(ti, carry, re=re, im=im, lr=lr, li=li):
            xr, xi = carry
            r0 = pl.multiple_of(ti * nb, nb)
            nr = lr * xr - li * xi + s_ref[pl.ds(r0, nb), re]
            ni = lr * xi + li * xr + s_ref[pl.ds(r0, nb), im]
            s_ref[pl.ds(r0, nb), re] = nr
            s_ref[pl.ds(r0, nb), im] = ni
            return nr, ni

        xr, xi = lax.fori_loop(0, tt, step, (st_ref[:, re], st_ref[:, im]), unroll=4)
        st_ref[:, re] = xr
        st_ref[:, im] = xi

    width = u.shape[1]
    kw = ns * MXU_DIM // width
    for n in range(width // MXU_DIM):
        oc = slice(n * MXU_DIM, (n + 1) * MXU_DIM)
        xre = s_ref[:, n * kw:(n + 1) * kw].astype(BF16)
        xim = s_ref[:, ns + n * kw:ns + (n + 1) * kw].astype(BF16)
        y = _dot(xre, cre_ref[n]) - _dot(xim, cim_ref[n]) + d_ref[:, oc] * u[:, oc]
        y_ref[:, oc] = jax.nn.gelu(y).astype(BF16)


def _ssm(u2, bt, cre, cim, lre, lim, d_row, nb, tt):
    rows, width = u2.shape
    ns = lre.shape[1]
    blk = tt * nb
    return pl.pallas_call(
        functools.partial(_ssm_kernel, tt=tt, nb=nb, ns=ns, strip=4 * LANES),
        grid=(rows // blk,),
        in_specs=[pl.BlockSpec((blk, width), lambda i: (i, 0)),
                  _resident(bt.shape), _resident(cre.shape), _resident(cim.shape),
                  _resident(lre.shape), _resident(lim.shape), _resident(d_row.shape)],
        out_specs=pl.BlockSpec((blk, width), lambda i: (i, 0)),
        out_shape=jax.ShapeDtypeStruct((rows, width), BF16),
        scratch_shapes=[pltpu.VMEM((blk, 2 * ns), F32), pltpu.VMEM((nb, 2 * ns), F32)],
        compiler_params=pltpu.CompilerParams(
            dimension_semantics=("arbitrary",), vmem_limit_bytes=VMEM_LIMIT),
        name="ssm",
    )(u2, bt, cre, cim, lre, lim, d_row)


def _ssm_weights(l_re, l_im, bb_re, bb_im, c_re, c_im):
    g, p, h = bb_re.shape
    ns = g * p
    eye = jnp.eye(g, dtype=F32)
    b_full = jnp.concatenate(
        [jnp.einsum('gph,gk->ghkp', bb, eye).reshape(g * h, ns) for bb in (bb_re, bb_im)],
        axis=1).astype(BF16)
    n_tiles = 2 * ns // MXU_DIM
    tiles = []
    for n in range(n_tiles):
        kt = (n % (n_tiles // 2)) * MXU_DIM // (4 * LANES)
        tiles.append(b_full[kt * LANES:(kt + 1) * LANES, n * MXU_DIM:(n + 1) * MXU_DIM])
    bt = jnp.stack(tiles)
    width = g * h
    kw = ns * MXU_DIM // width

    def ctiles(c):
        full = jnp.einsum('ghp,gk->gpkh', c.astype(F32), eye).reshape(ns, width).astype(BF16)
        return jnp.stack([full[n * kw:(n + 1) * kw, n * MXU_DIM:(n + 1) * MXU_DIM]
                          for n in range(width // MXU_DIM)])

    return bt, ctiles(c_re), ctiles(c_im), l_re.reshape(1, ns), l_im.reshape(1, ns)


def _merge_kernel(x_ref, a_ref, y_ref, wg_ref, wsb_ref, wglu_ref, wo_ref, g_ref, b_ref,
                  o_ref, *, cw):
    x = x_ref[...]
    xb = x.astype(BF16)
    attn = a_ref[...]
    y = y_ref[...]
    d = x.shape[1]
    acc = DN_ALPHA * x
    for c in range(d // cw):
        cs = slice(c * cw, (c + 1) * cw)
        gs = slice(d + c * cw, d + (c + 1) * cw)
        attn_br = _dot(attn, wsb_ref[:, cs])
        ssm_br = _dot(y, wglu_ref[:, cs]) * jax.nn.sigmoid(_dot(y, wglu_ref[:, gs]))
        mixed = (jax.nn.sigmoid(_dot(xb, wg_ref[:, cs])) * attn_br
                 + jax.nn.sigmoid(_dot(xb, wg_ref[:, gs])) * ssm_br)
        acc = acc + _dot(mixed.astype(BF16), wo_ref[cs, :])
    o_ref[...] = _layer_norm(acc, g_ref[...], b_ref[...])


def _merge(x2, attn2, y_tm, w_g, w_sb, w_glu, w_o, ln_g, ln_b, t, tm):
    n, d = x2.shape
    sbw = attn2.shape[1]
    ssw = w_glu.shape[0]
    tpb = t // tm
    row = lambda i: (i, 0)
    return pl.pallas_call(
        functools.partial(_merge_kernel, cw=2 * MXU_DIM),
        grid=(n // tm,),
        in_specs=[pl.BlockSpec((tm, d), row), pl.BlockSpec((tm, sbw), row),
                  pl.BlockSpec((tm, ssw), lambda i: (i % tpb, i // tpb)),
                  _resident(w_g.shape), _resident(w_sb.shape), _resident(w_glu.shape),
                  _resident(w_o.shape), _resident(ln_g.shape), _resident(ln_b.shape)],
        out_specs=pl.BlockSpec((tm, d), row),
        out_shape=jax.ShapeDtypeStruct((n, d), F32),
        compiler_params=pltpu.CompilerParams(
            dimension_semantics=("parallel",), vmem_limit_bytes=VMEM_LIMIT),
        name="merge",
    )(x2, attn2, y_tm, w_g, w_sb, w_glu, w_o, ln_g, ln_b)


def _ffn_kernel(h_ref, halo_ref, p_ref, wup_ref, cw_ref, cb_ref, wdn_ref, wpe_ref, wpg_ref,
                g_ref, b_ref, o_ref, acc_ref, *, tpb, n_chunks, halo):
    h = h_ref[...]
    hb = h.astype(BF16)
    first = pl.program_id(0) % tpb == 0
    hb_ext = jnp.concatenate(
        [jnp.where(first, 0.0, halo_ref[...]).astype(BF16), hb], axis=0)

    def conv_up(j):
        up = _dot(hb_ext, wup_ref[j])
        w = cw_ref[j]
        out = (w[2:3] * up + w[1:2] * pltpu.roll(up, 1, 0) + w[0:1] * pltpu.roll(up, 2, 0)
               + cb_ref[j])
        return out[halo:]

    acc_ref[...] = DN_ALPHA * h + _dot(p_ref[...].astype(BF16), wpe_ref[...]) * jax.nn.sigmoid(
        _dot(hb, wpg_ref[...]))

    def chunk(j, carry):
        val = conv_up(j)
        gate = conv_up(n_chunks + j)
        act = (gate * jax.nn.sigmoid(gate) * val).astype(BF16)
        acc_ref[...] += _dot(act, wdn_ref[j])
        return carry

    lax.fori_loop(0, n_chunks, chunk, 0)
    o_ref[...] = _layer_norm(acc_ref[...], g_ref[...], b_ref[...])


def _ffn(h1, p2, w_up_t, conv_w_t, conv_b_t, w_dn_t, w_pe, w_pg, ln_g, ln_b, t, tm):
    n, d = h1.shape
    pd = p2.shape[1]
    halo = BF16_SUBLANES
    tpb = t // tm
    n_chunks = w_dn_t.shape[0]
    row = lambda i: (i, 0)
    return pl.pallas_call(
        functools.partial(_ffn_kernel, tpb=tpb, n_chunks=n_chunks, halo=halo),
        grid=(n // tm,),
        in_specs=[pl.BlockSpec((tm, d), row),
                  pl.BlockSpec((halo, d), lambda i: (jnp.maximum(i * (tm // halo) - 1, 0), 0)),
                  pl.BlockSpec((tm, pd), row),
                  _resident(w_up_t.shape), _resident(conv_w_t.shape), _resident(conv_b_t.shape),
                  _resident(w_dn_t.shape), _resident(w_pe.shape), _resident(w_pg.shape),
                  _resident(ln_g.shape), _resident(ln_b.shape)],
        out_specs=pl.BlockSpec((tm, d), row),
        out_shape=jax.ShapeDtypeStruct((n, d), F32),
        scratch_shapes=[pltpu.VMEM((tm, d), F32)],
        compiler_params=pltpu.CompilerParams(
            dimension_semantics=("parallel",), vmem_limit_bytes=VMEM_LIMIT),
        name="ffn",
    )(h1, h1, p2, w_up_t, conv_w_t, conv_b_t, w_dn_t, w_pe, w_pg, ln_g, ln_b)


def kernel(x, p, w_in, w_sb_out, ssm_a_re, ssm_a_im, ssm_log_step, ssm_b_re, ssm_b_im,
           ssm_c_re, ssm_c_im, ssm_d, w_glu, w_o, ln1_g, ln1_b, w_up, conv_w, conv_b,
           w_down, w_pe, w_pe_gate, ln2_g, ln2_b):
    b, t, d = x.shape
    assert w_in.shape[0] == DEPTH
    sbw = N_SB_HEADS * SB_HEAD_DIM
    ssw = w_glu.shape[1]
    dff = w_down.shape[1]
    n = b * t
    tm = min(512, t)
    tq = min(MXU_DIM, t)
    tt = min(32, t)
    assert t % tm == 0 and t % tq == 0 and t % tt == 0 and dff % MXU_DIM == 0
    n_chunks = dff // MXU_DIM

    h = x.reshape(n, d)
    for i in range(DEPTH):
        wi = w_in[i].astype(BF16)
        w_qkvu, w_gates = wi[:, :3 * sbw + ssw], wi[:, 3 * sbw + ssw:]

        q, k, v, u_tm = _proj(h, w_qkvu, b, t, tm, sbw, ssw)
        attn = _attention(q.reshape(b, t, sbw), k.reshape(b, t, sbw), v.reshape(b, t, sbw), tq,
                          ATTN_HEADS_PER_STEP)

        l_re, l_im, bb_re, bb_im = _ssm_prep(ssm_a_re[i], ssm_a_im[i], ssm_log_step[i],
                                             ssm_b_re[i], ssm_b_im[i])
        bt, cre, cim, lre, lim = _ssm_weights(l_re, l_im, bb_re, bb_im, ssm_c_re[i], ssm_c_im[i])
        y2 = _ssm(u_tm.reshape(t * b, ssw), bt, cre, cim, lre, lim,
                  ssm_d[i].reshape(1, ssw).astype(F32), b, tt)

        h = _merge(h, attn.reshape(n, sbw), y2.reshape(t, b * ssw), w_gates,
                   w_sb_out[i].astype(BF16), w_glu[i].astype(BF16), w_o[i].astype(BF16),
                   ln1_g[i].reshape(1, d), ln1_b[i].reshape(1, d), t, tm)

        w_up_t = w_up[i].astype(BF16).reshape(d, 2 * n_chunks, MXU_DIM).transpose(1, 0, 2)
        conv_w_t = conv_w[i].reshape(CONV_WIDTH, 2 * n_chunks, MXU_DIM).transpose(1, 0, 2)
        conv_b_t = conv_b[i].reshape(2 * n_chunks, 1, MXU_DIM)
        w_dn_t = w_down[i].astype(BF16).reshape(n_chunks, MXU_DIM, d)
        h = _ffn(h, p[i].reshape(n, -1), w_up_t, conv_w_t, conv_b_t, w_dn_t,
                 w_pe[i].astype(BF16), w_pe_gate[i].astype(BF16),
                 ln2_g[i].reshape(1, d), ln2_b[i].reshape(1, d), t, tm)
    return h.reshape(b, t, d)
```

```python
import functools
import math

import jax
import jax.numpy as jnp
from jax import lax
from jax.experimental import pallas as pl
from jax.experimental.pallas import tpu as pltpu

F32 = jnp.float32
BF16 = jnp.bfloat16

N_SB_HEADS = 8
SB_HEAD_DIM = 64
CONV_WIDTH = 3
DEPTH = 1
DN_ALPHA = (2.0 * DEPTH) ** 0.25
LN_EPS = 1e-5
LOG2E = 1.0 / math.log(2.0)

LANES = 128
MXU_DIM = 256
BF16_SUBLANES = 16

VMEM_LIMIT = 56 * 1024 * 1024
ATTN_HEADS_PER_STEP = 4


def _dot(a, b):
    return jnp.dot(a, b, preferred_element_type=F32)


def _dot_nt(a, b):
    return lax.dot_general(a, b, (((1,), (1,)), ((), ())), preferred_element_type=F32)


def _layer_norm(h, g, b):
    mu = jnp.mean(h, axis=-1, keepdims=True)
    d = h - mu
    var = jnp.mean(d * d, axis=-1, keepdims=True)
    return d * lax.rsqrt(var + LN_EPS) * g + b


def _resident(shape):
    nd = len(shape)
    return pl.BlockSpec(shape, lambda *_: (0,) * nd, pipeline_mode=pl.Buffered(1))


def _ssm_prep_kernel(are_ref, aim_ref, ls_ref, bre_ref, bim_ref,
                     lre_ref, lim_ref, bbre_ref, bbim_ref):
    a_re, a_im = are_ref[...], aim_ref[...]
    step = jnp.exp(ls_ref[...])
    mag = jnp.exp(a_re * step)
    l_re = mag * jnp.cos(a_im * step)
    l_im = mag * jnp.sin(a_im * step)
    n_re, n_im = l_re - 1.0, l_im
    den = a_re * a_re + a_im * a_im
    c_re = (n_re * a_re + n_im * a_im) / den
    c_im = (n_im * a_re - n_re * a_im) / den
    b_re, b_im = bre_ref[...], bim_ref[...]
    lre_ref[...] = l_re
    lim_ref[...] = l_im
    bbre_ref[...] = c_re * b_re - c_im * b_im
    bbim_ref[...] = c_re * b_im + c_im * b_re


def _ssm_prep(a_re, a_im, log_step, b_re, b_im):
    g, p = a_re.shape
    h = b_re.shape[-1]
    rows = g * p
    col = lambda a: a.reshape(rows, 1).astype(F32)
    ls = jnp.broadcast_to(log_step[:, None], (g, p))
    out = pl.pallas_call(
        _ssm_prep_kernel,
        out_shape=[jax.ShapeDtypeStruct((rows, 1), F32)] * 2
        + [jax.ShapeDtypeStruct((rows, h), F32)] * 2,
        name="ssm_prep",
    )(col(a_re), col(a_im), col(ls), b_re.reshape(rows, h).astype(F32),
      b_im.reshape(rows, h).astype(F32))
    l_re, l_im, bb_re, bb_im = out
    return (l_re.reshape(g, p), l_im.reshape(g, p),
            bb_re.reshape(g, p, h), bb_im.reshape(g, p, h))


def _proj_kernel(x_ref, w_ref, q_ref, k_ref, v_ref, u_ref, *, sbw, scale):
    xb = x_ref[...].astype(BF16)
    q_ref[...] = (_dot(xb, w_ref[:, 0:sbw]) * scale).astype(BF16)
    k_ref[...] = _dot(xb, w_ref[:, sbw:2 * sbw]).astype(BF16)
    v_ref[...] = _dot(xb, w_ref[:, 2 * sbw:3 * sbw]).astype(BF16)
    u_ref[...] = _dot(xb, w_ref[:, 3 * sbw:])


def _proj(x2, w_qkvu, b, t, tm, sbw, ssw):
    n, d = x2.shape
    tpb = t // tm
    row = lambda i: (i, 0)
    return pl.pallas_call(
        functools.partial(_proj_kernel, sbw=sbw, scale=1.0 / math.sqrt(SB_HEAD_DIM)),
        grid=(n // tm,),
        in_specs=[pl.BlockSpec((tm, d), row), _resident(w_qkvu.shape)],
        out_specs=[pl.BlockSpec((tm, sbw), row)] * 3
        + [pl.BlockSpec((tm, ssw), lambda i: (i % tpb, i // tpb))],
        out_shape=[jax.ShapeDtypeStruct((n, sbw), BF16)] * 3
        + [jax.ShapeDtypeStruct((t, b * ssw), F32)],
        compiler_params=pltpu.CompilerParams(
            dimension_semantics=("parallel",), vmem_limit_bytes=VMEM_LIMIT),
        name="proj",
    )(x2, w_qkvu)


def _attn_kernel(q_ref, k_ref, v_ref, tri_ref, o_ref, z_ref, acc_ref, c_ref, *, tq, heads):
    qi = pl.program_id(2)
    neg_tri = tri_ref[...]
    rows = lax.broadcasted_iota(jnp.int32, (tq, tq), 0)
    cols = lax.broadcasted_iota(jnp.int32, (tq, tq), 1)
    causal = cols < rows
    hd = SB_HEAD_DIM

    pairs = heads // 2
    lane = lax.broadcasted_iota(jnp.int32, (tq, LANES), 1)
    low_half = lane < hd

    sls = [slice(h * hd, (h + 1) * hd) for h in range(heads)]

    def scores(k0):
        return [_dot_nt(q_ref[0, :, sl], k_ref[0, pl.ds(k0, tq), sl]) for sl in sls]

    def tile(k0, next_k0, mask):
        zs = [z_ref[h] for h in range(heads)]
        sps = []
        for z in zs:
            sp = jnp.maximum(z, 0.0) + jnp.log(1.0 + jnp.exp2(jnp.abs(z) * -LOG2E))
            sps.append(sp if mask is None else jnp.where(mask, sp, 0.0))
        args = [(z - sp) + _dot(sp.astype(BF16), neg_tri) for z, sp in zip(zs, sps)]
        for h, z in enumerate(scores(next_k0)):
            z_ref[h] = z
        ws = []
        for arg in args:
            w = jnp.exp(arg)
            ws.append((w if mask is None else jnp.where(mask, w, 0.0)).astype(BF16))
        for pr in range(pairs):
            vp = v_ref[0, pl.ds(k0, tq), pr * LANES:(pr + 1) * LANES]
            v2 = jnp.concatenate([jnp.where(low_half, vp, 0), jnp.where(low_half, 0, vp)], axis=0)
            pv = _dot(jnp.concatenate(ws[2 * pr:2 * pr + 2], axis=1), v2)
            sums = jnp.where(low_half, jnp.sum(sps[2 * pr], axis=-1, keepdims=True),
                             jnp.sum(sps[2 * pr + 1], axis=-1, keepdims=True))
            if mask is not None:
                acc_ref[pr] = pv
                c_ref[pr] = sums
            else:
                c = c_ref[pr]
                acc_ref[pr] += jnp.exp(-c) * pv
                c_ref[pr] = c + sums

    def block_start(j):
        return pl.multiple_of(jnp.maximum(j, 0) * tq, tq)

    for h, z in enumerate(scores(block_start(qi))):
        z_ref[h] = z
    tile(block_start(qi), block_start(qi - 1), causal)

    def body(i, carry):
        j = qi - 1 - i
        tile(block_start(j), block_start(j - 1), None)
        return carry

    lax.fori_loop(0, qi, body, 0)
    o_ref[0] = jnp.concatenate([acc_ref[pr] for pr in range(pairs)], axis=-1).astype(BF16)


def _attention(q, k, v, tq, heads):
    b, t, w = q.shape
    lanes = heads * SB_HEAD_DIM
    tri = -(jnp.arange(tq)[:, None] > jnp.arange(tq)[None, :]).astype(BF16)
    qspec = pl.BlockSpec((1, tq, lanes), lambda bi, hp, qi: (bi, qi, hp))
    kvspec = pl.BlockSpec((1, t, lanes), lambda bi, hp, qi: (bi, 0, hp))
    return pl.pallas_call(
        functools.partial(_attn_kernel, tq=tq, heads=heads),
        grid=(b, w // lanes, t // tq),
        in_specs=[qspec, kvspec, kvspec, pl.BlockSpec((tq, tq), lambda bi, hp, qi: (0, 0))],
        out_specs=qspec,
        out_shape=jax.ShapeDtypeStruct((b, t, w), BF16),
        scratch_shapes=[pltpu.VMEM((heads, tq, tq), F32),
                        pltpu.VMEM((heads // 2, tq, LANES), F32),
                        pltpu.VMEM((heads // 2, tq, LANES), F32)],
        compiler_params=pltpu.CompilerParams(
            dimension_semantics=("parallel", "parallel", "arbitrary"),
            vmem_limit_bytes=VMEM_LIMIT),
        name="attn",
    )(q, k, v, tri)


def _ssm_kernel(u_ref, bt_ref, cre_ref, cim_ref, lre_ref, lim_ref, d_ref, y_ref,
                s_ref, st_ref, *, tt, nb, ns, strip):
    @pl.when(pl.program_id(0) == 0)
    def _():
        st_ref[...] = jnp.zeros_like(st_ref)

    u = u_ref[...]
    ub = u.astype(BF16)
    n_tiles = 2 * ns // MXU_DIM
    for n in range(n_tiles):
        kt = (n % (n_tiles // 2)) * MXU_DIM // (4 * LANES)
        s_ref[:, n * MXU_DIM:(n + 1) * MXU_DIM] = _dot(
            ub[:, kt * LANES:(kt + 1) * LANES], bt_ref[n])

    for s in range(ns // strip):
        re = slice(s * strip, (s + 1) * strip)
        im = slice(ns + s * strip, ns + (s + 1) * strip)
        lr = jnp.broadcast_to(lre_ref[:, re], (nb, strip))
        li = jnp.broadcast_to(lim_ref[:, re], (nb, strip))

        def step(ti, carry, re=re, im=im, lr=lr, li=li):
            xr, xi = carry
            r0 = pl.multiple_of(ti * nb, nb)
            nr = lr * xr - li * xi + s_ref[pl.ds(r0, nb), re]
            ni = lr * xi + li * xr + s_ref[pl.ds(r0, nb), im]
            s_ref[pl.ds(r0, nb), re] = nr
            s_ref[pl.ds(r0, nb), im] = ni
            return nr, ni

        xr, xi = lax.fori_loop(0, tt, step, (st_ref[:, re], st_ref[:, im]), unroll=4)
        st_ref[:, re] = xr
        st_ref[:, im] = xi

    width = u.shape[1]
    kw = ns * MXU_DIM // width
    for n in range(width // MXU_DIM):
        oc = slice(n * MXU_DIM, (n + 1) * MXU_DIM)
        xre = s_ref[:, n * kw:(n + 1) * kw].astype(BF16)
        xim = s_ref[:, ns + n * kw:ns + (n + 1) * kw].astype(BF16)
        y = _dot(xre, cre_ref[n]) - _dot(xim, cim_ref[n]) + d_ref[:, oc] * u[:, oc]
        y_ref[:, oc] = jax.nn.gelu(y).astype(BF16)


def _ssm(u2, bt, cre, cim, lre, lim, d_row, nb, tt):
    rows, width = u2.shape
    ns = lre.shape[1]
    blk = tt * nb
    return pl.pallas_call(
        functools.partial(_ssm_kernel, tt=tt, nb=nb, ns=ns, strip=4 * LANES),
        grid=(rows // blk,),
        in_specs=[pl.BlockSpec((blk, width), lambda i: (i, 0)),
                  _resident(bt.shape), _resident(cre.shape), _resident(cim.shape),
                  _resident(lre.shape), _resident(lim.shape), _resident(d_row.shape)],
        out_specs=pl.BlockSpec((blk, width), lambda i: (i, 0)),
        out_shape=jax.ShapeDtypeStruct((rows, width), BF16),
        scratch_shapes=[pltpu.VMEM((blk, 2 * ns), F32), pltpu.VMEM((nb, 2 * ns), F32)],
        compiler_params=pltpu.CompilerParams(
            dimension_semantics=("arbitrary",), vmem_limit_bytes=VMEM_LIMIT),
        name="ssm",
    )(u2, bt, cre, cim, lre, lim, d_row)


def _ssm_weights(l_re, l_im, bb_re, bb_im, c_re, c_im):
    g, p, h = bb_re.shape
    ns = g * p
    eye = jnp.eye(g, dtype=F32)
    b_full = jnp.concatenate(
        [jnp.einsum('gph,gk->ghkp', bb, eye).reshape(g * h, ns) for bb in (bb_re, bb_im)],
        axis=1).astype(BF16)
    n_tiles = 2 * ns // MXU_DIM
    tiles = []
    for n in range(n_tiles):
        kt = (n % (n_tiles // 2)) * MXU_DIM // (4 * LANES)
        tiles.append(b_full[kt * LANES:(kt + 1) * LANES, n * MXU_DIM:(n + 1) * MXU_DIM])
    bt = jnp.stack(tiles)
    width = g * h
    kw = ns * MXU_DIM // width

    def ctiles(c):
        full = jnp.einsum('ghp,gk->gpkh', c.astype(F32), eye).reshape(ns, width).astype(BF16)
        return jnp.stack([full[n * kw:(n + 1) * kw, n * MXU_DIM:(n + 1) * MXU_DIM]
                          for n in range(width // MXU_DIM)])

    return bt, ctiles(c_re), ctiles(c_im), l_re.reshape(1, ns), l_im.reshape(1, ns)


def _merge_kernel(x_ref, a_ref, y_ref, wg_ref, wsb_ref, wglu_ref, wo_ref, g_ref, b_ref,
                  o_ref, *, cw):
    x = x_ref[...]
    xb = x.astype(BF16)
    attn = a_ref[...]
    y = y_ref[...]
    d = x.shape[1]
    n_chunks = d // cw

    def branch_dots(c):
        cs = slice(c * cw, (c + 1) * cw)
        gs = slice(d + c * cw, d + (c + 1) * cw)
        return (_dot(attn, wsb_ref[:, cs]), _dot(y, wglu_ref[:, cs]), _dot(y, wglu_ref[:, gs]),
                _dot(xb, wg_ref[:, cs]), _dot(xb, wg_ref[:, gs]))

    def mix(dots):
        attn_br, glu_val, glu_gate, gate_attn, gate_ssm = dots
        ssm_br = glu_val * jax.nn.sigmoid(glu_gate)
        return (jax.nn.sigmoid(gate_attn) * attn_br
                + jax.nn.sigmoid(gate_ssm) * ssm_br).astype(BF16)

    acc = DN_ALPHA * x
    dots = branch_dots(0)
    for c in range(n_chunks):
        mixed = mix(dots)
        if c + 1 < n_chunks:
            dots = branch_dots(c + 1)
        acc = acc + _dot(mixed, wo_ref[c * cw:(c + 1) * cw, :])
    o_ref[...] = _layer_norm(acc, g_ref[...], b_ref[...])


def _merge(x2, attn2, y_tm, w_g, w_sb, w_glu, w_o, ln_g, ln_b, t, tm):
    n, d = x2.shape
    sbw = attn2.shape[1]
    ssw = w_glu.shape[0]
    tpb = t // tm
    row = lambda i: (i, 0)
    return pl.pallas_call(
        functools.partial(_merge_kernel, cw=MXU_DIM),
        grid=(n // tm,),
        in_specs=[pl.BlockSpec((tm, d), row), pl.BlockSpec((tm, sbw), row),
                  pl.BlockSpec((tm, ssw), lambda i: (i % tpb, i // tpb)),
                  _resident(w_g.shape), _resident(w_sb.shape), _resident(w_glu.shape),
                  _resident(w_o.shape), _resident(ln_g.shape), _resident(ln_b.shape)],
        out_specs=pl.BlockSpec((tm, d), row),
        out_shape=jax.ShapeDtypeStruct((n, d), F32),
        compiler_params=pltpu.CompilerParams(
            dimension_semantics=("parallel",), vmem_limit_bytes=VMEM_LIMIT),
        name="merge",
    )(x2, attn2, y_tm, w_g, w_sb, w_glu, w_o, ln_g, ln_b)


def _ffn_kernel(h_ref, halo_ref, p_ref, wup_ref, cw_ref, cb_ref, wdn_ref, wpe_ref, wpg_ref,
                g_ref, b_ref, o_ref, up_ref, act_ref, *, tpb, n_chunks, halo):
    h = h_ref[...]
    hb = h.astype(BF16)
    tm = h.shape[0]
    first = pl.program_id(0) % tpb == 0
    hb_ext = jnp.concatenate(
        [jnp.where(first, 0.0, halo_ref[...]).astype(BF16), hb], axis=0)

    def up_proj(j):
        slot = j % 2
        up_ref[slot, 0] = _dot(hb_ext, wup_ref[j])
        up_ref[slot, 1] = _dot(hb_ext, wup_ref[n_chunks + j])

    def conv(slot, part, j):
        w = cw_ref[j]
        return (w[2:3] * up_ref[slot, part, pl.ds(halo, tm), :]
                + w[1:2] * up_ref[slot, part, pl.ds(halo - 1, tm), :]
                + (w[0:1] * up_ref[slot, part, pl.ds(halo - 2, tm), :] + cb_ref[j]))

    def activation(j):
        val = conv(j % 2, 0, j)
        gate = conv(j % 2, 1, n_chunks + j)
        act_ref[j] = (gate * jax.nn.sigmoid(gate) * val).astype(BF16)

    up_proj(0)
    res = DN_ALPHA * h + _dot(p_ref[...].astype(BF16), wpe_ref[...]) * jax.nn.sigmoid(
        _dot(hb, wpg_ref[...]))

    for j in range(n_chunks - 1):
        up_proj(j + 1)
        activation(j)
    activation(n_chunks - 1)
    for j in range(n_chunks):
        res = res + _dot(act_ref[j], wdn_ref[j])
    o_ref[...] = _layer_norm(res, g_ref[...], b_ref[...])


def _ffn(h1, p2, w_up_t, conv_w_t, conv_b_t, w_dn_t, w_pe, w_pg, ln_g, ln_b, t, tm):
    n, d = h1.shape
    pd = p2.shape[1]
    halo = BF16_SUBLANES
    tpb = t // tm
    n_chunks = w_dn_t.shape[0]
    row = lambda i: (i, 0)
    return pl.pallas_call(
        functools.partial(_ffn_kernel, tpb=tpb, n_chunks=n_chunks, halo=halo),
        grid=(n // tm,),
        in_specs=[pl.BlockSpec((tm, d), row),
                  pl.BlockSpec((halo, d), lambda i: (jnp.maximum(i * (tm // halo) - 1, 0), 0)),
                  pl.BlockSpec((tm, pd), row),
                  _resident(w_up_t.shape), _resident(conv_w_t.shape), _resident(conv_b_t.shape),
                  _resident(w_dn_t.shape), _resident(w_pe.shape), _resident(w_pg.shape),
                  _resident(ln_g.shape), _resident(ln_b.shape)],
        out_specs=pl.BlockSpec((tm, d), row),
        out_shape=jax.ShapeDtypeStruct((n, d), F32),
        scratch_shapes=[pltpu.VMEM((2, 2, tm + halo, MXU_DIM), F32),
                        pltpu.VMEM((n_chunks, tm, MXU_DIM), BF16)],
        compiler_params=pltpu.CompilerParams(
            dimension_semantics=("parallel",), vmem_limit_bytes=VMEM_LIMIT),
        name="ffn",
    )(h1, h1, p2, w_up_t, conv_w_t, conv_b_t, w_dn_t, w_pe, w_pg, ln_g, ln_b)


def kernel(x, p, w_in, w_sb_out, ssm_a_re, ssm_a_im, ssm_log_step, ssm_b_re, ssm_b_im,
           ssm_c_re, ssm_c_im, ssm_d, w_glu, w_o, ln1_g, ln1_b, w_up, conv_w, conv_b,
           w_down, w_pe, w_pe_gate, ln2_g, ln2_b):
    b, t, d = x.shape
    assert w_in.shape[0] == DEPTH
    sbw = N_SB_HEADS * SB_HEAD_DIM
    ssw = w_glu.shape[1]
    dff = w_down.shape[1]
    n = b * t
    tm = min(512, t)
    tq = min(MXU_DIM, t)
    tt = min(32, t)
    assert t % tm == 0 and t % tq == 0 and t % tt == 0 and dff % MXU_DIM == 0
    n_chunks = dff // MXU_DIM

    h = x.reshape(n, d)
    for i in range(DEPTH):
        wi = w_in[i].astype(BF16)
        w_qkvu, w_gates = wi[:, :3 * sbw + ssw], wi[:, 3 * sbw + ssw:]

        q, k, v, u_tm = _proj(h, w_qkvu, b, t, tm, sbw, ssw)
        attn = _attention(q.reshape(b, t, sbw), k.reshape(b, t, sbw), v.reshape(b, t, sbw), tq,
                          ATTN_HEADS_PER_STEP)

        l_re, l_im, bb_re, bb_im = _ssm_prep(ssm_a_re[i], ssm_a_im[i], ssm_log_step[i],
                                             ssm_b_re[i], ssm_b_im[i])
        bt, cre, cim, lre, lim = _ssm_weights(l_re, l_im, bb_re, bb_im, ssm_c_re[i], ssm_c_im[i])
        y2 = _ssm(u_tm.reshape(t * b, ssw), bt, cre, cim, lre, lim,
                  ssm_d[i].reshape(1, ssw).astype(F32), b, tt)

        h = _merge(h, attn.reshape(n, sbw), y2.reshape(t, b * ssw), w_gates,
                   w_sb_out[i].astype(BF16), w_glu[i].astype(BF16), w_o[i].astype(BF16),
                   ln1_g[i].reshape(1, d), ln1_b[i].reshape(1, d), t, tm)

        w_up_t = w_up[i].astype(BF16).reshape(d, 2 * n_chunks, MXU_DIM).transpose(1, 0, 2)
        conv_w_t = conv_w[i].reshape(CONV_WIDTH, 2 * n_chunks, MXU_DIM).transpose(1, 0, 2)
        conv_b_t = conv_b[i].reshape(2 * n_chunks, 1, MXU_DIM)
        w_dn_t = w_down[i].astype(BF16).reshape(n_chunks, MXU_DIM, d)
        h = _ffn(h, p[i].reshape(n, -1), w_up_t, conv_w_t, conv_b_t, w_dn_t,
                 w_pe[i].astype(BF16), w_pe_gate[i].astype(BF16),
                 ln2_g[i].reshape(1, d), ln2_b[i].reshape(1, d), t, tm)
    return h.reshape(b, t, d)
```

```python
import functools
import math

import jax
import jax.numpy as jnp
from jax import lax
from jax.experimental import pallas as pl
from jax.experimental.pallas import tpu as pltpu

F32 = jnp.float32
BF16 = jnp.bfloat16

N_SB_HEADS = 8
SB_HEAD_DIM = 64
CONV_WIDTH = 3
DEPTH = 1
DN_ALPHA = (2.0 * DEPTH) ** 0.25
LN_EPS = 1e-5
LOG2E = 1.0 / math.log(2.0)

LANES = 128
MXU_DIM = 256
BF16_SUBLANES = 16

VMEM_LIMIT = 56 * 1024 * 1024
ATTN_HEADS_PER_STEP = 8


def _dot(a, b):
    return jnp.dot(a, b, preferred_element_type=F32)


def _dot_nt(a, b):
    return lax.dot_general(a, b, (((1,), (1,)), ((), ())), preferred_element_type=F32)


def _layer_norm(h, g, b):
    mu = jnp.mean(h, axis=-1, keepdims=True)
    d = h - mu
    var = jnp.mean(d * d, axis=-1, keepdims=True)
    return d * lax.rsqrt(var + LN_EPS) * g + b


def _resident(shape):
    nd = len(shape)
    return pl.BlockSpec(shape, lambda *_: (0,) * nd, pipeline_mode=pl.Buffered(1))


def _ssm_prep_kernel(are_ref, aim_ref, ls_ref, bre_ref, bim_ref,
                     lre_ref, lim_ref, bbre_ref, bbim_ref):
    a_re, a_im = are_ref[...], aim_ref[...]
    step = jnp.exp(ls_ref[...])
    mag = jnp.exp(a_re * step)
    l_re = mag * jnp.cos(a_im * step)
    l_im = mag * jnp.sin(a_im * step)
    n_re, n_im = l_re - 1.0, l_im
    den = a_re * a_re + a_im * a_im
    c_re = (n_re * a_re + n_im * a_im) / den
    c_im = (n_im * a_re - n_re * a_im) / den
    b_re, b_im = bre_ref[...], bim_ref[...]
    lre_ref[...] = l_re
    lim_ref[...] = l_im
    bbre_ref[...] = c_re * b_re - c_im * b_im
    bbim_ref[...] = c_re * b_im + c_im * b_re


def _ssm_prep(a_re, a_im, log_step, b_re, b_im):
    g, p = a_re.shape
    h = b_re.shape[-1]
    rows = g * p
    col = lambda a: a.reshape(rows, 1).astype(F32)
    ls = jnp.broadcast_to(log_step[:, None], (g, p))
    out = pl.pallas_call(
        _ssm_prep_kernel,
        out_shape=[jax.ShapeDtypeStruct((rows, 1), F32)] * 2
        + [jax.ShapeDtypeStruct((rows, h), F32)] * 2,
        name="ssm_prep",
    )(col(a_re), col(a_im), col(ls), b_re.reshape(rows, h).astype(F32),
      b_im.reshape(rows, h).astype(F32))
    l_re, l_im, bb_re, bb_im = out
    return (l_re.reshape(g, p), l_im.reshape(g, p),
            bb_re.reshape(g, p, h), bb_im.reshape(g, p, h))


def _proj_kernel(x_ref, w_ref, q_ref, k_ref, v_ref, u_ref, *, sbw, scale):
    xb = x_ref[...].astype(BF16)
    q_ref[...] = (_dot(xb, w_ref[:, 0:sbw]) * scale).astype(BF16)
    k_ref[...] = _dot(xb, w_ref[:, sbw:2 * sbw]).astype(BF16)
    v_ref[...] = _dot(xb, w_ref[:, 2 * sbw:3 * sbw]).astype(BF16)
    u_ref[...] = _dot(xb, w_ref[:, 3 * sbw:])


def _proj(x2, w_qkvu, tm, sbw, ssw):
    n, d = x2.shape
    row = lambda i: (i, 0)
    return pl.pallas_call(
        functools.partial(_proj_kernel, sbw=sbw, scale=1.0 / math.sqrt(SB_HEAD_DIM)),
        grid=(n // tm,),
        in_specs=[pl.BlockSpec((tm, d), row), _resident(w_qkvu.shape)],
        out_specs=[pl.BlockSpec((tm, sbw), row)] * 3 + [pl.BlockSpec((tm, ssw), row)],
        out_shape=[jax.ShapeDtypeStruct((n, sbw), BF16)] * 3
        + [jax.ShapeDtypeStruct((n, ssw), F32)],
        compiler_params=pltpu.CompilerParams(
            dimension_semantics=("parallel",), vmem_limit_bytes=VMEM_LIMIT),
        name="proj",
    )(x2, w_qkvu)


def _attn_kernel(q_ref, k_ref, v_ref, tri_ref, o_ref, z_ref, acc_ref, c_ref, *, tq, heads):
    qi = pl.program_id(2)
    neg_tri = tri_ref[...]
    rows = lax.broadcasted_iota(jnp.int32, (tq, tq), 0)
    cols = lax.broadcasted_iota(jnp.int32, (tq, tq), 1)
    causal = cols < rows
    hd = SB_HEAD_DIM

    pairs = heads // 2
    lane = lax.broadcasted_iota(jnp.int32, (tq, LANES), 1)
    low_half = lane < hd

    q_heads = []
    for pr in range(pairs):
        qp = q_ref[0, :, pr * LANES:(pr + 1) * LANES]
        q_heads += [jnp.where(low_half, qp, 0), jnp.where(low_half, 0, qp)]

    def scores(k0):
        return [_dot_nt(q_heads[h], k_ref[0, pl.ds(k0, tq), (h // 2) * LANES:(h // 2 + 1) * LANES])
                for h in range(heads)]

    def tile(k0, next_k0, mask):
        zs = [z_ref[h] for h in range(heads)]
        sps = []
        for z in zs:
            sp = jnp.maximum(z, 0.0) + jnp.log(1.0 + jnp.exp2(jnp.abs(z) * -LOG2E))
            sps.append(sp if mask is None else jnp.where(mask, sp, 0.0))
        args = [(z - sp) + _dot(sp.astype(BF16), neg_tri) for z, sp in zip(zs, sps)]
        for h, z in enumerate(scores(next_k0)):
            z_ref[h] = z
        ws = []
        for arg in args:
            w = jnp.exp(arg)
            ws.append((w if mask is None else jnp.where(mask, w, 0.0)).astype(BF16))
        for pr in range(pairs):
            vp = v_ref[0, pl.ds(k0, tq), pr * LANES:(pr + 1) * LANES]
            v2 = jnp.concatenate([jnp.where(low_half, vp, 0), jnp.where(low_half, 0, vp)], axis=0)
            pv = _dot(jnp.concatenate(ws[2 * pr:2 * pr + 2], axis=1), v2)
            sums = jnp.where(low_half, jnp.sum(sps[2 * pr], axis=-1, keepdims=True),
                             jnp.sum(sps[2 * pr + 1], axis=-1, keepdims=True))
            if mask is not None:
                acc_ref[pr] = pv
                c_ref[pr] = sums
            else:
                c = c_ref[pr]
                acc_ref[pr] += jnp.exp(-c) * pv
                c_ref[pr] = c + sums

    def block_start(j):
        return pl.multiple_of(jnp.maximum(j, 0) * tq, tq)

    for h, z in enumerate(scores(block_start(qi))):
        z_ref[h] = z
    tile(block_start(qi), block_start(qi - 1), causal)

    def body(i, carry):
        j = qi - 1 - i
        tile(block_start(j), block_start(j - 1), None)
        return carry

    lax.fori_loop(0, qi, body, 0)
    o_ref[0] = jnp.concatenate([acc_ref[pr] for pr in range(pairs)], axis=-1).astype(BF16)


def _attention(q, k, v, tq, heads):
    b, t, w = q.shape
    lanes = heads * SB_HEAD_DIM
    tri = -(jnp.arange(tq)[:, None] > jnp.arange(tq)[None, :]).astype(BF16)
    qspec = pl.BlockSpec((1, tq, lanes), lambda bi, hp, qi: (bi, qi, hp))
    kvspec = pl.BlockSpec((1, t, lanes), lambda bi, hp, qi: (bi, 0, hp))
    return pl.pallas_call(
        functools.partial(_attn_kernel, tq=tq, heads=heads),
        grid=(b, w // lanes, t // tq),
        in_specs=[qspec, kvspec, kvspec, pl.BlockSpec((tq, tq), lambda bi, hp, qi: (0, 0))],
        out_specs=qspec,
        out_shape=jax.ShapeDtypeStruct((b, t, w), BF16),
        scratch_shapes=[pltpu.VMEM((heads, tq, tq), F32),
                        pltpu.VMEM((heads // 2, tq, LANES), F32),
                        pltpu.VMEM((heads // 2, tq, LANES), F32)],
        compiler_params=pltpu.CompilerParams(
            dimension_semantics=("parallel", "parallel", "arbitrary"),
            vmem_limit_bytes=VMEM_LIMIT),
        name="attn",
    )(q, k, v, tri)


def _ssm_kernel(u_ref, bt_ref, cre_ref, cim_ref, lre_ref, lim_ref, d_ref, y_ref,
                s_ref, st_ref, ur_ref, yr_ref, *, tt, nb, ns, strip, pitch):
    @pl.when(pl.program_id(0) == 0)
    def _():
        st_ref[...] = jnp.zeros_like(st_ref)

    width = u_ref.shape[2]
    slabs = width // LANES
    for b in range(nb):
        for k in range(slabs):
            ur_ref[k, pl.ds(b, tt, stride=pitch), :] = u_ref[b, :, k * LANES:(k + 1) * LANES]
    u = jnp.concatenate(
        [jnp.concatenate([ur_ref[k, pitch * t:pitch * t + nb, :] for t in range(tt)], axis=0)
         for k in range(slabs)], axis=1)
    ub = u.astype(BF16)
    half_tiles = ns // MXU_DIM
    n_strips = ns // strip
    kw = ns * MXU_DIM // width

    def input_proj(s):
        for part in range(2):
            for n in range(s * strip // MXU_DIM, (s + 1) * strip // MXU_DIM):
                kt = n * MXU_DIM // (4 * LANES)
                c0 = part * ns + n * MXU_DIM
                s_ref[:, c0:c0 + MXU_DIM] = _dot(
                    ub[:, kt * LANES:(kt + 1) * LANES], bt_ref[part * half_tiles + n])

    def scan(s):
        re = slice(s * strip, (s + 1) * strip)
        im = slice(ns + s * strip, ns + (s + 1) * strip)
        lr = jnp.broadcast_to(lre_ref[:, re], (nb, strip))
        li = jnp.broadcast_to(lim_ref[:, re], (nb, strip))
        xr, xi = st_ref[:, re], st_ref[:, im]
        for t in range(tt):
            rows = slice(t * nb, (t + 1) * nb)
            xr, xi = (lr * xr - li * xi + s_ref[rows, re], lr * xi + li * xr + s_ref[rows, im])
            s_ref[rows, re] = xr
            s_ref[rows, im] = xi
        st_ref[:, re] = xr
        st_ref[:, im] = xi

    def output_proj(n):
        oc = slice(n * MXU_DIM, (n + 1) * MXU_DIM)
        xre = s_ref[:, n * kw:(n + 1) * kw].astype(BF16)
        xim = s_ref[:, ns + n * kw:ns + (n + 1) * kw].astype(BF16)
        y = jax.nn.gelu(_dot(xre, cre_ref[n]) - _dot(xim, cim_ref[n]) + d_ref[:, oc] * u[:, oc])
        for kk in range(MXU_DIM // LANES):
            for t in range(tt):
                yr_ref[n * (MXU_DIM // LANES) + kk, pitch * t:pitch * t + nb, :] = (
                    y[t * nb:(t + 1) * nb, kk * LANES:(kk + 1) * LANES])

    input_proj(0)
    for s in range(n_strips):
        if s + 1 < n_strips:
            input_proj(s + 1)
        scan(s)
        if (s + 1) * strip % kw == 0:
            output_proj((s + 1) * strip // kw - 1)
    for b in range(nb):
        for k in range(slabs):
            y_ref[b, :, k * LANES:(k + 1) * LANES] = (
                yr_ref[k, pl.ds(b, tt, stride=pitch), :].astype(BF16))


def _ssm(u3, bt, cre, cim, lre, lim, d_row, tt):
    nb, t, width = u3.shape
    ns = lre.shape[1]
    blk = tt * nb
    pitch = -(-nb // 8) * 8
    if (pitch // 8) % 2 == 0:
        pitch += 8
    block = pl.BlockSpec((nb, tt, width), lambda i: (0, i, 0))
    relayout = pltpu.VMEM((width // LANES, tt * pitch, LANES), F32)
    return pl.pallas_call(
        functools.partial(_ssm_kernel, tt=tt, nb=nb, ns=ns, strip=4 * LANES, pitch=pitch),
        grid=(t // tt,),
        in_specs=[block,
                  _resident(bt.shape), _resident(cre.shape), _resident(cim.shape),
                  _resident(lre.shape), _resident(lim.shape), _resident(d_row.shape)],
        out_specs=block,
        out_shape=jax.ShapeDtypeStruct((nb, t, width), BF16),
        scratch_shapes=[pltpu.VMEM((blk, 2 * ns), F32), pltpu.VMEM((nb, 2 * ns), F32),
                        relayout, relayout],
        compiler_params=pltpu.CompilerParams(
            dimension_semantics=("arbitrary",), vmem_limit_bytes=VMEM_LIMIT),
        name="ssm",
    )(u3, bt, cre, cim, lre, lim, d_row)


def _ssm_weights(l_re, l_im, bb_re, bb_im, c_re, c_im):
    g, p, h = bb_re.shape
    ns = g * p
    eye = jnp.eye(g, dtype=F32)
    b_full = jnp.concatenate(
        [jnp.einsum('gph,gk->ghkp', bb, eye).reshape(g * h, ns) for bb in (bb_re, bb_im)],
        axis=1).astype(BF16)
    n_tiles = 2 * ns // MXU_DIM
    tiles = []
    for n in range(n_tiles):
        kt = (n % (n_tiles // 2)) * MXU_DIM // (4 * LANES)
        tiles.append(b_full[kt * LANES:(kt + 1) * LANES, n * MXU_DIM:(n + 1) * MXU_DIM])
    bt = jnp.stack(tiles)
    width = g * h
    kw = ns * MXU_DIM // width

    def ctiles(c):
        full = jnp.einsum('ghp,gk->gpkh', c.astype(F32), eye).reshape(ns, width).astype(BF16)
        return jnp.stack([full[n * kw:(n + 1) * kw, n * MXU_DIM:(n + 1) * MXU_DIM]
                          for n in range(width // MXU_DIM)])

    return bt, ctiles(c_re), ctiles(c_im), l_re.reshape(1, ns), l_im.reshape(1, ns)


def _merge_kernel(x_ref, a_ref, y_ref, wg_ref, wsb_ref, wglu_ref, wo_ref, g_ref, b_ref,
                  o_ref, *, cw):
    x = x_ref[...]
    xb = x.astype(BF16)
    attn = a_ref[...]
    y = y_ref[...]
    d = x.shape[1]
    n_chunks = d // cw

    def branch_dots(c):
        cs = slice(c * cw, (c + 1) * cw)
        gs = slice(d + c * cw, d + (c + 1) * cw)
        return (_dot(attn, wsb_ref[:, cs]), _dot(y, wglu_ref[:, cs]), _dot(y, wglu_ref[:, gs]),
                _dot(xb, wg_ref[:, cs]), _dot(xb, wg_ref[:, gs]))

    def mix(dots):
        attn_br, glu_val, glu_gate, gate_attn, gate_ssm = dots
        ssm_br = glu_val * jax.nn.sigmoid(glu_gate)
        return (jax.nn.sigmoid(gate_attn) * attn_br
                + jax.nn.sigmoid(gate_ssm) * ssm_br).astype(BF16)

    acc = DN_ALPHA * x
    dots = branch_dots(0)
    for c in range(n_chunks):
        mixed = mix(dots)
        if c + 1 < n_chunks:
            dots = branch_dots(c + 1)
        acc = acc + _dot(mixed, wo_ref[c * cw:(c + 1) * cw, :])
    o_ref[...] = _layer_norm(acc, g_ref[...], b_ref[...])


def _merge(x2, attn2, y2, w_g, w_sb, w_glu, w_o, ln_g, ln_b, tm):
    n, d = x2.shape
    sbw = attn2.shape[1]
    ssw = w_glu.shape[0]
    row = lambda i: (i, 0)
    return pl.pallas_call(
        functools.partial(_merge_kernel, cw=MXU_DIM),
        grid=(n // tm,),
        in_specs=[pl.BlockSpec((tm, d), row), pl.BlockSpec((tm, sbw), row),
                  pl.BlockSpec((tm, ssw), row),
                  _resident(w_g.shape), _resident(w_sb.shape), _resident(w_glu.shape),
                  _resident(w_o.shape), _resident(ln_g.shape), _resident(ln_b.shape)],
        out_specs=pl.BlockSpec((tm, d), row),
        out_shape=jax.ShapeDtypeStruct((n, d), F32),
        compiler_params=pltpu.CompilerParams(
            dimension_semantics=("parallel",), vmem_limit_bytes=VMEM_LIMIT),
        name="merge",
    )(x2, attn2, y2, w_g, w_sb, w_glu, w_o, ln_g, ln_b)


def _ffn_kernel(h_ref, halo_ref, p_ref, wup_ref, cw_ref, cb_ref, wdn_ref, wpe_ref, wpg_ref,
                g_ref, b_ref, o_ref, up_ref, act_ref, *, tpb, n_chunks, halo):
    h = h_ref[...]
    hb = h.astype(BF16)
    tm = h.shape[0]
    first = pl.program_id(0) % tpb == 0
    hb_ext = jnp.concatenate(
        [jnp.where(first, 0.0, halo_ref[...]).astype(BF16), hb], axis=0)

    cw = MXU_DIM
    dff = n_chunks * cw

    def up_proj(j):
        for part in range(2):
            c0 = part * dff + j * cw
            up_ref[j % 2, part] = _dot(hb_ext, wup_ref[:, c0:c0 + cw])

    def conv(slot, part, c0):
        w = cw_ref[:, c0:c0 + cw]
        return (w[2:3] * up_ref[slot, part, pl.ds(halo, tm), :]
                + w[1:2] * up_ref[slot, part, pl.ds(halo - 1, tm), :]
                + (w[0:1] * up_ref[slot, part, pl.ds(halo - 2, tm), :] + cb_ref[:, c0:c0 + cw]))

    def activation(j):
        val = conv(j % 2, 0, j * cw)
        gate = conv(j % 2, 1, dff + j * cw)
        act_ref[j] = (gate * jax.nn.sigmoid(gate) * val).astype(BF16)

    up_proj(0)
    res = DN_ALPHA * h + _dot(p_ref[...].astype(BF16), wpe_ref[...]) * jax.nn.sigmoid(
        _dot(hb, wpg_ref[...]))

    for j in range(n_chunks - 1):
        up_proj(j + 1)
        activation(j)
    activation(n_chunks - 1)
    for j in range(n_chunks):
        res = res + _dot(act_ref[j], wdn_ref[j * cw:(j + 1) * cw, :])
    o_ref[...] = _layer_norm(res, g_ref[...], b_ref[...])


def _ffn(h1, p2, w_up_t, conv_w_t, conv_b_t, w_dn_t, w_pe, w_pg, ln_g, ln_b, t, tm):
    n, d = h1.shape
    pd = p2.shape[1]
    halo = BF16_SUBLANES
    tpb = t // tm
    n_chunks = w_dn_t.shape[0] // MXU_DIM
    row = lambda i: (i, 0)
    return pl.pallas_call(
        functools.partial(_ffn_kernel, tpb=tpb, n_chunks=n_chunks, halo=halo),
        grid=(n // tm,),
        in_specs=[pl.BlockSpec((tm, d), row),
                  pl.BlockSpec((halo, d), lambda i: (jnp.maximum(i * (tm // halo) - 1, 0), 0)),
                  pl.BlockSpec((tm, pd), row),
                  _resident(w_up_t.shape), _resident(conv_w_t.shape), _resident(conv_b_t.shape),
                  _resident(w_dn_t.shape), _resident(w_pe.shape), _resident(w_pg.shape),
                  _resident(ln_g.shape), _resident(ln_b.shape)],
        out_specs=pl.BlockSpec((tm, d), row),
        out_shape=jax.ShapeDtypeStruct((n, d), F32),
        scratch_shapes=[pltpu.VMEM((2, 2, tm + halo, MXU_DIM), F32),
                        pltpu.VMEM((n_chunks, tm, MXU_DIM), BF16)],
        compiler_params=pltpu.CompilerParams(
            dimension_semantics=("parallel",), vmem_limit_bytes=VMEM_LIMIT),
        name="ffn",
    )(h1, h1, p2, w_up_t, conv_w_t, conv_b_t, w_dn_t, w_pe, w_pg, ln_g, ln_b)


def kernel(x, p, w_in, w_sb_out, ssm_a_re, ssm_a_im, ssm_log_step, ssm_b_re, ssm_b_im,
           ssm_c_re, ssm_c_im, ssm_d, w_glu, w_o, ln1_g, ln1_b, w_up, conv_w, conv_b,
           w_down, w_pe, w_pe_gate, ln2_g, ln2_b):
    b, t, d = x.shape
    assert w_in.shape[0] == DEPTH
    sbw = N_SB_HEADS * SB_HEAD_DIM
    ssw = w_glu.shape[1]
    dff = w_down.shape[1]
    n = b * t
    tm = min(512, t)
    tq = min(MXU_DIM, t)
    tt = min(32, t)
    assert t % tm == 0 and t % tq == 0 and t % tt == 0 and dff % MXU_DIM == 0
    n_chunks = dff // MXU_DIM

    h = x.reshape(n, d)
    for i in range(DEPTH):
        wi = w_in[i].astype(BF16)
        w_qkvu, w_gates = wi[:, :3 * sbw + ssw], wi[:, 3 * sbw + ssw:]

        q, k, v, u = _proj(h, w_qkvu, tm, sbw, ssw)
        attn = _attention(q.reshape(b, t, sbw), k.reshape(b, t, sbw), v.reshape(b, t, sbw), tq,
                          ATTN_HEADS_PER_STEP)

        l_re, l_im, bb_re, bb_im = _ssm_prep(ssm_a_re[i], ssm_a_im[i], ssm_log_step[i],
                                             ssm_b_re[i], ssm_b_im[i])
        bt, cre, cim, lre, lim = _ssm_weights(l_re, l_im, bb_re, bb_im, ssm_c_re[i], ssm_c_im[i])
        y = _ssm(u.reshape(b, t, ssw), bt, cre, cim, lre, lim,
                 ssm_d[i].reshape(1, ssw).astype(F32), tt)

        h = _merge(h, attn.reshape(n, sbw), y.reshape(n, ssw), w_gates,
                   w_sb_out[i].astype(BF16), w_glu[i].astype(BF16), w_o[i].astype(BF16),
                   ln1_g[i].reshape(1, d), ln1_b[i].reshape(1, d), tm)

        h = _ffn(h, p[i].reshape(n, -1), w_up[i].astype(BF16), conv_w[i],
                 conv_b[i].reshape(1, -1), w_down[i].astype(BF16),
                 w_pe[i].astype(BF16), w_pe_gate[i].astype(BF16),
                 ln2_g[i].reshape(1, d), ln2_b[i].reshape(1, d), t, tm)
    return h.reshape(b, t, d)
```

```python
import functools
import math

import jax
import jax.numpy as jnp
from jax import lax
from jax.experimental import pallas as pl
from jax.experimental.pallas import tpu as pltpu

F32 = jnp.float32
BF16 = jnp.bfloat16

N_SB_HEADS = 8
SB_HEAD_DIM = 64
CONV_WIDTH = 3
DEPTH = 1
DN_ALPHA = (2.0 * DEPTH) ** 0.25
LN_EPS = 1e-5
LOG2E = 1.0 / math.log(2.0)
EXP_UNDERFLOW = 150.0 * math.log(2.0) + 1.0

LANES = 128
MXU_DIM = 256
BF16_SUBLANES = 16

VMEM_LIMIT = 56 * 1024 * 1024
ATTN_HEADS_PER_STEP = 8


def _dot(a, b):
    return jnp.dot(a, b, preferred_element_type=F32)


def _dot_nt(a, b):
    return lax.dot_general(a, b, (((1,), (1,)), ((), ())), preferred_element_type=F32)


def _layer_norm(h, g, b):
    mu = jnp.mean(h, axis=-1, keepdims=True)
    d = h - mu
    var = jnp.mean(d * d, axis=-1, keepdims=True)
    return d * lax.rsqrt(var + LN_EPS) * g + b


def _resident(shape):
    nd = len(shape)
    return pl.BlockSpec(shape, lambda *_: (0,) * nd, pipeline_mode=pl.Buffered(1))


def _ssm_prep_kernel(are_ref, aim_ref, ls_ref, bre_ref, bim_ref,
                     lre_ref, lim_ref, bbre_ref, bbim_ref):
    a_re, a_im = are_ref[...], aim_ref[...]
    step = jnp.exp(ls_ref[...])
    mag = jnp.exp(a_re * step)
    l_re = mag * jnp.cos(a_im * step)
    l_im = mag * jnp.sin(a_im * step)
    n_re, n_im = l_re - 1.0, l_im
    den = a_re * a_re + a_im * a_im
    c_re = (n_re * a_re + n_im * a_im) / den
    c_im = (n_im * a_re - n_re * a_im) / den
    b_re, b_im = bre_ref[...], bim_ref[...]
    lre_ref[...] = l_re
    lim_ref[...] = l_im
    bbre_ref[...] = c_re * b_re - c_im * b_im
    bbim_ref[...] = c_re * b_im + c_im * b_re


def _ssm_prep(a_re, a_im, log_step, b_re, b_im):
    g, p = a_re.shape
    h = b_re.shape[-1]
    rows = g * p
    col = lambda a: a.reshape(rows, 1).astype(F32)
    ls = jnp.broadcast_to(log_step[:, None], (g, p))
    out = pl.pallas_call(
        _ssm_prep_kernel,
        out_shape=[jax.ShapeDtypeStruct((rows, 1), F32)] * 2
        + [jax.ShapeDtypeStruct((rows, h), F32)] * 2,
        name="ssm_prep",
    )(col(a_re), col(a_im), col(ls), b_re.reshape(rows, h).astype(F32),
      b_im.reshape(rows, h).astype(F32))
    l_re, l_im, bb_re, bb_im = out
    return (l_re.reshape(g, p), l_im.reshape(g, p),
            bb_re.reshape(g, p, h), bb_im.reshape(g, p, h))


def _proj_kernel(x_ref, w_ref, q_ref, k_ref, v_ref, u_ref, *, sbw, scale):
    xb = x_ref[...].astype(BF16)
    q_ref[...] = (_dot(xb, w_ref[:, 0:sbw]) * scale).astype(BF16)
    k_ref[...] = _dot(xb, w_ref[:, sbw:2 * sbw]).astype(BF16)
    v_ref[...] = _dot(xb, w_ref[:, 2 * sbw:3 * sbw]).astype(BF16)
    u_ref[...] = _dot(xb, w_ref[:, 3 * sbw:])


def _proj(x2, w_qkvu, tm, sbw, ssw):
    n, d = x2.shape
    row = lambda i: (i, 0)
    return pl.pallas_call(
        functools.partial(_proj_kernel, sbw=sbw, scale=1.0 / math.sqrt(SB_HEAD_DIM)),
        grid=(n // tm,),
        in_specs=[pl.BlockSpec((tm, d), row), _resident(w_qkvu.shape)],
        out_specs=[pl.BlockSpec((tm, sbw), row)] * 3 + [pl.BlockSpec((tm, ssw), row)],
        out_shape=[jax.ShapeDtypeStruct((n, sbw), BF16)] * 3
        + [jax.ShapeDtypeStruct((n, ssw), F32)],
        compiler_params=pltpu.CompilerParams(
            dimension_semantics=("parallel",), vmem_limit_bytes=VMEM_LIMIT),
        name="proj",
    )(x2, w_qkvu)


def _attn_kernel(q_ref, k_ref, v_ref, tri_ref, o_ref, z_ref, acc_ref, c_ref, *, tq, heads):
    qi = pl.program_id(2)
    neg_tri = tri_ref[...]
    rows = lax.broadcasted_iota(jnp.int32, (tq, tq), 0)
    cols = lax.broadcasted_iota(jnp.int32, (tq, tq), 1)
    causal = cols < rows
    hd = SB_HEAD_DIM

    pairs = heads // 2
    lane = lax.broadcasted_iota(jnp.int32, (tq, LANES), 1)
    low_half = lane < hd

    q_heads = []
    for pr in range(pairs):
        qp = q_ref[0, :, pr * LANES:(pr + 1) * LANES]
        q_heads += [jnp.where(low_half, qp, 0), jnp.where(low_half, 0, qp)]

    def scores(k0):
        return [_dot_nt(q_heads[h], k_ref[0, pl.ds(k0, tq), (h // 2) * LANES:(h // 2 + 1) * LANES])
                for h in range(heads)]

    def tile(k0, next_k0, mask):
        zs = [z_ref[h] for h in range(heads)]
        sps = []
        for z in zs:
            sp = jnp.maximum(z, 0.0) + jnp.log(1.0 + jnp.exp2(jnp.abs(z) * -LOG2E))
            sps.append(sp if mask is None else jnp.where(mask, sp, 0.0))
        args = [(z - sp) + _dot(sp.astype(BF16), neg_tri) for z, sp in zip(zs, sps)]
        for h, z in enumerate(scores(next_k0)):
            z_ref[h] = z
        ws = []
        for arg in args:
            w = jnp.exp(arg)
            ws.append((w if mask is None else jnp.where(mask, w, 0.0)).astype(BF16))
        for pr in range(pairs):
            vp = v_ref[0, pl.ds(k0, tq), pr * LANES:(pr + 1) * LANES]
            v2 = jnp.concatenate([jnp.where(low_half, vp, 0), jnp.where(low_half, 0, vp)], axis=0)
            pv = _dot(jnp.concatenate(ws[2 * pr:2 * pr + 2], axis=1), v2)
            sums = jnp.where(low_half, jnp.sum(sps[2 * pr], axis=-1, keepdims=True),
                             jnp.sum(sps[2 * pr + 1], axis=-1, keepdims=True))
            if mask is not None:
                acc_ref[pr] = pv
                c_ref[pr] = sums
            else:
                c = c_ref[pr]
                acc_ref[pr] += jnp.exp(-c) * pv
                c_ref[pr] = c + sums

    def block_start(j):
        return pl.multiple_of(jnp.maximum(j, 0) * tq, tq)

    for h, z in enumerate(scores(block_start(qi))):
        z_ref[h] = z
    tile(block_start(qi), block_start(qi - 1), causal)

    def min_carry():
        m = c_ref[0]
        for pr in range(1, pairs):
            m = jnp.minimum(m, c_ref[pr])
        return jnp.min(m)

    def more(state):
        i, cmin = state
        return jnp.logical_and(i < qi, cmin < EXP_UNDERFLOW)

    def body(state):
        i, _ = state
        j = qi - 1 - i
        tile(block_start(j), block_start(j - 1), None)
        return i + 1, min_carry()

    lax.while_loop(more, body, (jnp.int32(0), min_carry()))
    o_ref[0] = jnp.concatenate([acc_ref[pr] for pr in range(pairs)], axis=-1).astype(BF16)


def _attention(q, k, v, tq, heads):
    b, t, w = q.shape
    lanes = heads * SB_HEAD_DIM
    tri = -(jnp.arange(tq)[:, None] > jnp.arange(tq)[None, :]).astype(BF16)
    qspec = pl.BlockSpec((1, tq, lanes), lambda bi, hp, qi: (bi, qi, hp))
    kvspec = pl.BlockSpec((1, t, lanes), lambda bi, hp, qi: (bi, 0, hp))
    return pl.pallas_call(
        functools.partial(_attn_kernel, tq=tq, heads=heads),
        grid=(b, w // lanes, t // tq),
        in_specs=[qspec, kvspec, kvspec, pl.BlockSpec((tq, tq), lambda bi, hp, qi: (0, 0))],
        out_specs=qspec,
        out_shape=jax.ShapeDtypeStruct((b, t, w), BF16),
        scratch_shapes=[pltpu.VMEM((heads, tq, tq), F32),
                        pltpu.VMEM((heads // 2, tq, LANES), F32),
                        pltpu.VMEM((heads // 2, tq, LANES), F32)],
        compiler_params=pltpu.CompilerParams(
            dimension_semantics=("parallel", "parallel", "arbitrary"),
            vmem_limit_bytes=VMEM_LIMIT),
        name="attn",
    )(q, k, v, tri)


def _ssm_kernel(u_ref, bt_ref, cre_ref, cim_ref, lre_ref, lim_ref, d_ref, y_ref,
                s_ref, st_ref, ur_ref, yr_ref, *, tt, nb, ns, strip, pitch):
    @pl.when(pl.program_id(0) == 0)
    def _():
        st_ref[...] = jnp.zeros_like(st_ref)

    width = u_ref.shape[2]
    slabs = width // LANES
    for b in range(nb):
        for k in range(slabs):
            ur_ref[k, pl.ds(b, tt, stride=pitch), :] = u_ref[b, :, k * LANES:(k + 1) * LANES]
    u = jnp.concatenate(
        [jnp.concatenate([ur_ref[k, pitch * t:pitch * t + nb, :] for t in range(tt)], axis=0)
         for k in range(slabs)], axis=1)
    ub = u.astype(BF16)
    half_tiles = ns // MXU_DIM
    n_strips = ns // strip
    kw = ns * MXU_DIM // width

    def input_proj(s):
        for part in range(2):
            for n in range(s * strip // MXU_DIM, (s + 1) * strip // MXU_DIM):
                kt = n * MXU_DIM // (4 * LANES)
                c0 = part * ns + n * MXU_DIM
                s_ref[:, c0:c0 + MXU_DIM] = _dot(
                    ub[:, kt * LANES:(kt + 1) * LANES], bt_ref[part * half_tiles + n])

    def scan(s):
        re = slice(s * strip, (s + 1) * strip)
        im = slice(ns + s * strip, ns + (s + 1) * strip)
        lr = jnp.broadcast_to(lre_ref[:, re], (nb, strip))
        li = jnp.broadcast_to(lim_ref[:, re], (nb, strip))
        xr, xi = st_ref[:, re], st_ref[:, im]
        for t in range(tt):
            rows = slice(t * nb, (t + 1) * nb)
            xr, xi = (lr * xr - li * xi + s_ref[rows, re], lr * xi + li * xr + s_ref[rows, im])
            s_ref[rows, re] = xr
            s_ref[rows, im] = xi
        st_ref[:, re] = xr
        st_ref[:, im] = xi

    def output_proj(n):
        oc = slice(n * MXU_DIM, (n + 1) * MXU_DIM)
        xre = s_ref[:, n * kw:(n + 1) * kw].astype(BF16)
        xim = s_ref[:, ns + n * kw:ns + (n + 1) * kw].astype(BF16)
        y = jax.nn.gelu(_dot(xre, cre_ref[n]) - _dot(xim, cim_ref[n]) + d_ref[:, oc] * u[:, oc])
        for kk in range(MXU_DIM // LANES):
            for t in range(tt):
                yr_ref[n * (MXU_DIM // LANES) + kk, pitch * t:pitch * t + nb, :] = (
                    y[t * nb:(t + 1) * nb, kk * LANES:(kk + 1) * LANES])

    input_proj(0)
    for s in range(n_strips):
        if s + 1 < n_strips:
            input_proj(s + 1)
        scan(s)
        if (s + 1) * strip % kw == 0:
            output_proj((s + 1) * strip // kw - 1)
    for b in range(nb):
        for k in range(slabs):
            y_ref[b, :, k * LANES:(k + 1) * LANES] = (
                yr_ref[k, pl.ds(b, tt, stride=pitch), :].astype(BF16))


def _ssm(u3, bt, cre, cim, lre, lim, d_row, tt):
    nb, t, width = u3.shape
    ns = lre.shape[1]
    blk = tt * nb
    pitch = -(-nb // 8) * 8
    if (pitch // 8) % 2 == 0:
        pitch += 8
    block = pl.BlockSpec((nb, tt, width), lambda i: (0, i, 0))
    relayout = pltpu.VMEM((width // LANES, tt * pitch, LANES), F32)
    return pl.pallas_call(
        functools.partial(_ssm_kernel, tt=tt, nb=nb, ns=ns, strip=4 * LANES, pitch=pitch),
        grid=(t // tt,),
        in_specs=[block,
                  _resident(bt.shape), _resident(cre.shape), _resident(cim.shape),
                  _resident(lre.shape), _resident(lim.shape), _resident(d_row.shape)],
        out_specs=block,
        out_shape=jax.ShapeDtypeStruct((nb, t, width), BF16),
        scratch_shapes=[pltpu.VMEM((blk, 2 * ns), F32), pltpu.VMEM((nb, 2 * ns), F32),
                        relayout, relayout],
        compiler_params=pltpu.CompilerParams(
            dimension_semantics=("arbitrary",), vmem_limit_bytes=VMEM_LIMIT),
        name="ssm",
    )(u3, bt, cre, cim, lre, lim, d_row)


def _ssm_weights(l_re, l_im, bb_re, bb_im, c_re, c_im):
    g, p, h = bb_re.shape
    ns = g * p
    eye = jnp.eye(g, dtype=F32)
    b_full = jnp.concatenate(
        [jnp.einsum('gph,gk->ghkp', bb, eye).reshape(g * h, ns) for bb in (bb_re, bb_im)],
        axis=1).astype(BF16)
    n_tiles = 2 * ns // MXU_DIM
    tiles = []
    for n in range(n_tiles):
        kt = (n % (n_tiles // 2)) * MXU_DIM // (4 * LANES)
        tiles.append(b_full[kt * LANES:(kt + 1) * LANES, n * MXU_DIM:(n + 1) * MXU_DIM])
    bt = jnp.stack(tiles)
    width = g * h
    kw = ns * MXU_DIM // width

    def ctiles(c):
        full = jnp.einsum('ghp,gk->gpkh', c.astype(F32), eye).reshape(ns, width).astype(BF16)
        return jnp.stack([full[n * kw:(n + 1) * kw, n * MXU_DIM:(n + 1) * MXU_DIM]
                          for n in range(width // MXU_DIM)])

    return bt, ctiles(c_re), ctiles(c_im), l_re.reshape(1, ns), l_im.reshape(1, ns)


def _merge_kernel(x_ref, a_ref, y_ref, wg_ref, wsb_ref, wglu_ref, wo_ref, g_ref, b_ref,
                  o_ref, *, cw):
    x = x_ref[...]
    xb = x.astype(BF16)
    attn = a_ref[...]
    y = y_ref[...]
    d = x.shape[1]
    n_chunks = d // cw

    def branch_dots(c):
        cs = slice(c * cw, (c + 1) * cw)
        gs = slice(d + c * cw, d + (c + 1) * cw)
        return (_dot(attn, wsb_ref[:, cs]), _dot(y, wglu_ref[:, cs]), _dot(y, wglu_ref[:, gs]),
                _dot(xb, wg_ref[:, cs]), _dot(xb, wg_ref[:, gs]))

    def mix(dots):
        attn_br, glu_val, glu_gate, gate_attn, gate_ssm = dots
        ssm_br = glu_val * jax.nn.sigmoid(glu_gate)
        return (jax.nn.sigmoid(gate_attn) * attn_br
                + jax.nn.sigmoid(gate_ssm) * ssm_br).astype(BF16)

    acc = DN_ALPHA * x
    dots = branch_dots(0)
    for c in range(n_chunks):
        mixed = mix(dots)
        if c + 1 < n_chunks:
            dots = branch_dots(c + 1)
        acc = acc + _dot(mixed, wo_ref[c * cw:(c + 1) * cw, :])
    o_ref[...] = _layer_norm(acc, g_ref[...], b_ref[...])


def _merge(x2, attn2, y2, w_g, w_sb, w_glu, w_o, ln_g, ln_b, tm):
    n, d = x2.shape
    sbw = attn2.shape[1]
    ssw = w_glu.shape[0]
    row = lambda i: (i, 0)
    return pl.pallas_call(
        functools.partial(_merge_kernel, cw=MXU_DIM),
        grid=(n // tm,),
        in_specs=[pl.BlockSpec((tm, d), row), pl.BlockSpec((tm, sbw), row),
                  pl.BlockSpec((tm, ssw), row),
                  _resident(w_g.shape), _resident(w_sb.shape), _resident(w_glu.shape),
                  _resident(w_o.shape), _resident(ln_g.shape), _resident(ln_b.shape)],
        out_specs=pl.BlockSpec((tm, d), row),
        out_shape=jax.ShapeDtypeStruct((n, d), F32),
        compiler_params=pltpu.CompilerParams(
            dimension_semantics=("parallel",), vmem_limit_bytes=VMEM_LIMIT),
        name="merge",
    )(x2, attn2, y2, w_g, w_sb, w_glu, w_o, ln_g, ln_b)


def _ffn_kernel(h_ref, halo_ref, p_ref, wup_ref, cw_ref, cb_ref, wdn_ref, wpe_ref, wpg_ref,
                g_ref, b_ref, o_ref, up_ref, act_ref, *, tpb, n_chunks, halo):
    h = h_ref[...]
    hb = h.astype(BF16)
    tm = h.shape[0]
    first = pl.program_id(0) % tpb == 0
    hb_ext = jnp.concatenate(
        [jnp.where(first, 0.0, halo_ref[...]).astype(BF16), hb], axis=0)

    cw = MXU_DIM
    dff = n_chunks * cw

    def up_proj(j):
        for part in range(2):
            c0 = part * dff + j * cw
            up_ref[j % 2, part] = _dot(hb_ext, wup_ref[:, c0:c0 + cw])

    def conv(slot, part, c0):
        w = cw_ref[:, c0:c0 + cw]
        return (w[2:3] * up_ref[slot, part, pl.ds(halo, tm), :]
                + w[1:2] * up_ref[slot, part, pl.ds(halo - 1, tm), :]
                + (w[0:1] * up_ref[slot, part, pl.ds(halo - 2, tm), :] + cb_ref[:, c0:c0 + cw]))

    def activation(j):
        val = conv(j % 2, 0, j * cw)
        gate = conv(j % 2, 1, dff + j * cw)
        act_ref[j] = (gate * jax.nn.sigmoid(gate) * val).astype(BF16)

    up_proj(0)
    res = DN_ALPHA * h + _dot(p_ref[...].astype(BF16), wpe_ref[...]) * jax.nn.sigmoid(
        _dot(hb, wpg_ref[...]))

    for j in range(n_chunks - 1):
        up_proj(j + 1)
        activation(j)
    activation(n_chunks - 1)
    for j in range(n_chunks):
        res = res + _dot(act_ref[j], wdn_ref[j * cw:(j + 1) * cw, :])
    o_ref[...] = _layer_norm(res, g_ref[...], b_ref[...])


def _ffn(h1, p2, w_up_t, conv_w_t, conv_b_t, w_dn_t, w_pe, w_pg, ln_g, ln_b, t, tm):
    n, d = h1.shape
    pd = p2.shape[1]
    halo = BF16_SUBLANES
    tpb = t // tm
    n_chunks = w_dn_t.shape[0] // MXU_DIM
    row = lambda i: (i, 0)
    return pl.pallas_call(
        functools.partial(_ffn_kernel, tpb=tpb, n_chunks=n_chunks, halo=halo),
        grid=(n // tm,),
        in_specs=[pl.BlockSpec((tm, d), row),
                  pl.BlockSpec((halo, d), lambda i: (jnp.maximum(i * (tm // halo) - 1, 0), 0)),
                  pl.BlockSpec((tm, pd), row),
                  _resident(w_up_t.shape), _resident(conv_w_t.shape), _resident(conv_b_t.shape),
                  _resident(w_dn_t.shape), _resident(w_pe.shape), _resident(w_pg.shape),
                  _resident(ln_g.shape), _resident(ln_b.shape)],
        out_specs=pl.BlockSpec((tm, d), row),
        out_shape=jax.ShapeDtypeStruct((n, d), F32),
        scratch_shapes=[pltpu.VMEM((2, 2, tm + halo, MXU_DIM), F32),
                        pltpu.VMEM((n_chunks, tm, MXU_DIM), BF16)],
        compiler_params=pltpu.CompilerParams(
            dimension_semantics=("parallel",), vmem_limit_bytes=VMEM_LIMIT),
        name="ffn",
    )(h1, h1, p2, w_up_t, conv_w_t, conv_b_t, w_dn_t, w_pe, w_pg, ln_g, ln_b)


def kernel(x, p, w_in, w_sb_out, ssm_a_re, ssm_a_im, ssm_log_step, ssm_b_re, ssm_b_im,
           ssm_c_re, ssm_c_im, ssm_d, w_glu, w_o, ln1_g, ln1_b, w_up, conv_w, conv_b,
           w_down, w_pe, w_pe_gate, ln2_g, ln2_b):
    b, t, d = x.shape
    assert w_in.shape[0] == DEPTH
    sbw = N_SB_HEADS * SB_HEAD_DIM
    ssw = w_glu.shape[1]
    dff = w_down.shape[1]
    n = b * t
    tm = min(512, t)
    tq = min(MXU_DIM, t)
    tt = min(32, t)
    assert t % tm == 0 and t % tq == 0 and t % tt == 0 and dff % MXU_DIM == 0
    n_chunks = dff // MXU_DIM

    h = x.reshape(n, d)
    for i in range(DEPTH):
        wi = w_in[i].astype(BF16)
        w_qkvu, w_gates = wi[:, :3 * sbw + ssw], wi[:, 3 * sbw + ssw:]

        q, k, v, u = _proj(h, w_qkvu, tm, sbw, ssw)
        attn = _attention(q.reshape(b, t, sbw), k.reshape(b, t, sbw), v.reshape(b, t, sbw), tq,
                          ATTN_HEADS_PER_STEP)

        l_re, l_im, bb_re, bb_im = _ssm_prep(ssm_a_re[i], ssm_a_im[i], ssm_log_step[i],
                                             ssm_b_re[i], ssm_b_im[i])
        bt, cre, cim, lre, lim = _ssm_weights(l_re, l_im, bb_re, bb_im, ssm_c_re[i], ssm_c_im[i])
        y = _ssm(u.reshape(b, t, ssw), bt, cre, cim, lre, lim,
                 ssm_d[i].reshape(1, ssw).astype(F32), tt)

        h = _merge(h, attn.reshape(n, sbw), y.reshape(n, ssw), w_gates,
                   w_sb_out[i].astype(BF16), w_glu[i].astype(BF16), w_o[i].astype(BF16),
                   ln1_g[i].reshape(1, d), ln1_b[i].reshape(1, d), tm)

        h = _ffn(h, p[i].reshape(n, -1), w_up[i].astype(BF16), conv_w[i],
                 conv_b[i].reshape(1, -1), w_down[i].astype(BF16),
                 w_pe[i].astype(BF16), w_pe_gate[i].astype(BF16),
                 ln2_g[i].reshape(1, d), ln2_b[i].reshape(1, d), t, tm)
    return h.reshape(b, t, d)
```

```python
import functools
import math

import jax
import jax.numpy as jnp
from jax import lax
from jax.experimental import pallas as pl
from jax.experimental.pallas import tpu as pltpu

F32 = jnp.float32
BF16 = jnp.bfloat16

N_SB_HEADS = 8
SB_HEAD_DIM = 64
CONV_WIDTH = 3
DEPTH = 1
DN_ALPHA = (2.0 * DEPTH) ** 0.25
LN_EPS = 1e-5
LOG2E = 1.0 / math.log(2.0)
EXP_UNDERFLOW = 150.0 * math.log(2.0) + 1.0

LANES = 128
MXU_DIM = 256
BF16_SUBLANES = 16

VMEM_LIMIT = 56 * 1024 * 1024
ATTN_HEADS_PER_STEP = 8


def _dot(a, b):
    return jnp.dot(a, b, preferred_element_type=F32)


def _dot_nt(a, b):
    return lax.dot_general(a, b, (((1,), (1,)), ((), ())), preferred_element_type=F32)


def _layer_norm(h, g, b):
    mu = jnp.mean(h, axis=-1, keepdims=True)
    d = h - mu
    var = jnp.mean(d * d, axis=-1, keepdims=True)
    return d * lax.rsqrt(var + LN_EPS) * g + b


def _resident(shape):
    nd = len(shape)
    return pl.BlockSpec(shape, lambda *_: (0,) * nd, pipeline_mode=pl.Buffered(1))


def _ssm_prep_kernel(are_ref, aim_ref, ls_ref, bre_ref, bim_ref,
                     lre_ref, lim_ref, bbre_ref, bbim_ref, *, h):
    a_re, a_im = are_ref[...], aim_ref[...]
    step = jnp.exp(ls_ref[...])
    mag = jnp.exp(a_re * step)
    l_re = mag * jnp.cos(a_im * step)
    l_im = mag * jnp.sin(a_im * step)
    n_re, n_im = l_re - 1.0, l_im
    den = a_re * a_re + a_im * a_im
    c_re = jnp.tile((n_re * a_re + n_im * a_im) / den, (h, 1))
    c_im = jnp.tile((n_im * a_re - n_re * a_im) / den, (h, 1))
    b_re, b_im = bre_ref[...], bim_ref[...]
    lre_ref[...] = l_re
    lim_ref[...] = l_im
    bbre_ref[...] = c_re * b_re - c_im * b_im
    bbim_ref[...] = c_re * b_im + c_im * b_re


def _ssm_prep(a_re, a_im, log_step, b_re, b_im):
    g, p = a_re.shape
    h = b_re.shape[-1]
    chan_major = lambda b: b.astype(F32).transpose(2, 0, 1).reshape(h * g, p)
    l_re, l_im, bb_re, bb_im = pl.pallas_call(
        functools.partial(_ssm_prep_kernel, h=h),
        out_shape=[jax.ShapeDtypeStruct((g, p), F32)] * 2
        + [jax.ShapeDtypeStruct((h * g, p), F32)] * 2,
        name="ssm_prep",
    )(a_re.astype(F32), a_im.astype(F32), log_step.reshape(g, 1).astype(F32),
      chan_major(b_re), chan_major(b_im))
    return l_re, l_im, bb_re.reshape(h, g, p), bb_im.reshape(h, g, p)


def _proj_kernel(x_ref, w_ref, q_ref, k_ref, v_ref, u_ref, *, sbw, scale):
    xb = x_ref[...].astype(BF16)
    q_ref[...] = (_dot(xb, w_ref[:, 0:sbw]) * scale).astype(BF16)
    k_ref[...] = _dot(xb, w_ref[:, sbw:2 * sbw]).astype(BF16)
    v_ref[...] = _dot(xb, w_ref[:, 2 * sbw:3 * sbw]).astype(BF16)
    u_ref[...] = _dot(xb, w_ref[:, 3 * sbw:])


def _proj(x2, w_qkvu, tm, sbw, ssw):
    n, d = x2.shape
    row = lambda i: (i, 0)
    return pl.pallas_call(
        functools.partial(_proj_kernel, sbw=sbw, scale=1.0 / math.sqrt(SB_HEAD_DIM)),
        grid=(n // tm,),
        in_specs=[pl.BlockSpec((tm, d), row), _resident(w_qkvu.shape)],
        out_specs=[pl.BlockSpec((tm, sbw), row)] * 3 + [pl.BlockSpec((tm, ssw), row)],
        out_shape=[jax.ShapeDtypeStruct((n, sbw), BF16)] * 3
        + [jax.ShapeDtypeStruct((n, ssw), F32)],
        compiler_params=pltpu.CompilerParams(
            dimension_semantics=("parallel",), vmem_limit_bytes=VMEM_LIMIT),
        name="proj",
    )(x2, w_qkvu)


def _attn_kernel(q_ref, k_ref, v_ref, tri_ref, o_ref, z_ref, acc_ref, c_ref, *, tq, heads):
    qi = pl.program_id(2)
    neg_tri = tri_ref[...]
    rows = lax.broadcasted_iota(jnp.int32, (tq, tq), 0)
    cols = lax.broadcasted_iota(jnp.int32, (tq, tq), 1)
    causal = cols < rows
    hd = SB_HEAD_DIM

    pairs = heads // 2
    lane = lax.broadcasted_iota(jnp.int32, (tq, LANES), 1)
    low_half = lane < hd

    q_heads = []
    for pr in range(pairs):
        qp = q_ref[0, :, pr * LANES:(pr + 1) * LANES]
        q_heads += [jnp.where(low_half, qp, 0), jnp.where(low_half, 0, qp)]

    def scores(k0):
        return [_dot_nt(q_heads[h], k_ref[0, pl.ds(k0, tq), (h // 2) * LANES:(h // 2 + 1) * LANES])
                for h in range(heads)]

    def tile(k0, next_k0, mask):
        zs = [z_ref[h] for h in range(heads)]
        sps = []
        for z in zs:
            sp = jnp.maximum(z, 0.0) + jnp.log(1.0 + jnp.exp2(jnp.abs(z) * -LOG2E))
            sps.append(sp if mask is None else jnp.where(mask, sp, 0.0))
        args = [(z - sp) + _dot(sp.astype(BF16), neg_tri) for z, sp in zip(zs, sps)]
        for h, z in enumerate(scores(next_k0)):
            z_ref[h] = z
        ws = []
        for arg in args:
            w = jnp.exp(arg)
            ws.append((w if mask is None else jnp.where(mask, w, 0.0)).astype(BF16))
        for pr in range(pairs):
            vp = v_ref[0, pl.ds(k0, tq), pr * LANES:(pr + 1) * LANES]
            v2 = jnp.concatenate([jnp.where(low_half, vp, 0), jnp.where(low_half, 0, vp)], axis=0)
            pv = _dot(jnp.concatenate(ws[2 * pr:2 * pr + 2], axis=1), v2)
            sums = jnp.where(low_half, jnp.sum(sps[2 * pr], axis=-1, keepdims=True),
                             jnp.sum(sps[2 * pr + 1], axis=-1, keepdims=True))
            if mask is not None:
                acc_ref[pr] = pv
                c_ref[pr] = sums
            else:
                c = c_ref[pr]
                acc_ref[pr] += jnp.exp(-c) * pv
                c_ref[pr] = c + sums

    def block_start(j):
        return pl.multiple_of(jnp.maximum(j, 0) * tq, tq)

    for h, z in enumerate(scores(block_start(qi))):
        z_ref[h] = z
    tile(block_start(qi), block_start(qi - 1), causal)

    def min_carry():
        m = c_ref[0]
        for pr in range(1, pairs):
            m = jnp.minimum(m, c_ref[pr])
        return jnp.min(m)

    def more(state):
        i, cmin = state
        return jnp.logical_and(i < qi, cmin < EXP_UNDERFLOW)

    def body(state):
        i, _ = state
        j = qi - 1 - i
        tile(block_start(j), block_start(j - 1), None)
        return i + 1, min_carry()

    lax.while_loop(more, body, (jnp.int32(0), min_carry()))
    o_ref[0] = jnp.concatenate([acc_ref[pr] for pr in range(pairs)], axis=-1).astype(BF16)


def _attention(q, k, v, tq, heads):
    b, t, w = q.shape
    lanes = heads * SB_HEAD_DIM
    tri = -(jnp.arange(tq)[:, None] > jnp.arange(tq)[None, :]).astype(BF16)
    qspec = pl.BlockSpec((1, tq, lanes), lambda bi, hp, qi: (bi, qi, hp))
    kvspec = pl.BlockSpec((1, t, lanes), lambda bi, hp, qi: (bi, 0, hp))
    return pl.pallas_call(
        functools.partial(_attn_kernel, tq=tq, heads=heads),
        grid=(b, w // lanes, t // tq),
        in_specs=[qspec, kvspec, kvspec, pl.BlockSpec((tq, tq), lambda bi, hp, qi: (0, 0))],
        out_specs=qspec,
        out_shape=jax.ShapeDtypeStruct((b, t, w), BF16),
        scratch_shapes=[pltpu.VMEM((heads, tq, tq), F32),
                        pltpu.VMEM((heads // 2, tq, LANES), F32),
                        pltpu.VMEM((heads // 2, tq, LANES), F32)],
        compiler_params=pltpu.CompilerParams(
            dimension_semantics=("parallel", "parallel", "arbitrary"),
            vmem_limit_bytes=VMEM_LIMIT),
        name="attn",
    )(q, k, v, tri)


def _ssm_kernel(u_ref, bt_ref, cre_ref, cim_ref, lre_ref, lim_ref, d_ref, y_ref,
                s_ref, st_ref, ur_ref, yr_ref, *, tt, nb, ns, strip, pitch):
    @pl.when(pl.program_id(0) == 0)
    def _():
        st_ref[...] = jnp.zeros_like(st_ref)

    width = u_ref.shape[2]
    slabs = width // LANES
    for b in range(nb):
        for k in range(slabs):
            ur_ref[k, pl.ds(b, tt, stride=pitch), :] = u_ref[b, :, k * LANES:(k + 1) * LANES]
    u = jnp.concatenate(
        [jnp.concatenate([ur_ref[k, pitch * t:pitch * t + nb, :] for t in range(tt)], axis=0)
         for k in range(slabs)], axis=1)
    ub = u.astype(BF16)
    half_tiles = ns // MXU_DIM
    n_strips = ns // strip
    kw = ns * MXU_DIM // width

    def input_proj(s):
        for part in range(2):
            for n in range(s * strip // MXU_DIM, (s + 1) * strip // MXU_DIM):
                kt = n * MXU_DIM // (4 * LANES)
                c0 = part * ns + n * MXU_DIM
                s_ref[:, c0:c0 + MXU_DIM] = _dot(
                    ub[:, kt * LANES:(kt + 1) * LANES], bt_ref[part * half_tiles + n])

    def scan(s):
        re = slice(s * strip, (s + 1) * strip)
        im = slice(ns + s * strip, ns + (s + 1) * strip)
        lr = jnp.broadcast_to(lre_ref[:, re], (nb, strip))
        li = jnp.broadcast_to(lim_ref[:, re], (nb, strip))
        xr, xi = st_ref[:, re], st_ref[:, im]
        for t in range(tt):
            rows = slice(t * nb, (t + 1) * nb)
            xr, xi = (lr * xr - li * xi + s_ref[rows, re], lr * xi + li * xr + s_ref[rows, im])
            s_ref[rows, re] = xr
            s_ref[rows, im] = xi
        st_ref[:, re] = xr
        st_ref[:, im] = xi

    def output_proj(n):
        oc = slice(n * MXU_DIM, (n + 1) * MXU_DIM)
        xre = s_ref[:, n * kw:(n + 1) * kw].astype(BF16)
        xim = s_ref[:, ns + n * kw:ns + (n + 1) * kw].astype(BF16)
        y = jax.nn.gelu(_dot(xre, cre_ref[n]) - _dot(xim, cim_ref[n]) + d_ref[:, oc] * u[:, oc])
        for kk in range(MXU_DIM // LANES):
            for t in range(tt):
                yr_ref[n * (MXU_DIM // LANES) + kk, pitch * t:pitch * t + nb, :] = (
                    y[t * nb:(t + 1) * nb, kk * LANES:(kk + 1) * LANES])

    input_proj(0)
    for s in range(n_strips):
        if s + 1 < n_strips:
            input_proj(s + 1)
        scan(s)
        if (s + 1) * strip % kw == 0:
            output_proj((s + 1) * strip // kw - 1)
    for b in range(nb):
        for k in range(slabs):
            y_ref[b, :, k * LANES:(k + 1) * LANES] = (
                yr_ref[k, pl.ds(b, tt, stride=pitch), :].astype(BF16))


def _ssm(u3, bt, cre, cim, lre, lim, d_row, tt):
    nb, t, width = u3.shape
    ns = lre.shape[1]
    blk = tt * nb
    pitch = -(-nb // 8) * 8
    if (pitch // 8) % 2 == 0:
        pitch += 8
    block = pl.BlockSpec((nb, tt, width), lambda i: (0, i, 0))
    relayout = pltpu.VMEM((width // LANES, tt * pitch, LANES), F32)
    return pl.pallas_call(
        functools.partial(_ssm_kernel, tt=tt, nb=nb, ns=ns, strip=4 * LANES, pitch=pitch),
        grid=(t // tt,),
        in_specs=[block,
                  _resident(bt.shape), _resident(cre.shape), _resident(cim.shape),
                  _resident(lre.shape), _resident(lim.shape), _resident(d_row.shape)],
        out_specs=block,
        out_shape=jax.ShapeDtypeStruct((nb, t, width), BF16),
        scratch_shapes=[pltpu.VMEM((blk, 2 * ns), F32), pltpu.VMEM((nb, 2 * ns), F32),
                        relayout, relayout],
        compiler_params=pltpu.CompilerParams(
            dimension_semantics=("arbitrary",), vmem_limit_bytes=VMEM_LIMIT),
        name="ssm",
    )(u3, bt, cre, cim, lre, lim, d_row)


def _ssm_weights(l_re, l_im, bb_re, bb_im, c_re, c_im):
    h, g, p = bb_re.shape
    ns = g * p
    gpt = MXU_DIM // p
    gps = LANES // h
    half_tiles = ns // MXU_DIM

    tile_id = jnp.arange(half_tiles)[:, None, None, None]
    slab_group = jnp.arange(gps)[None, :, None, None]
    col_group = (jnp.arange(MXU_DIM) // p)[None, None, None, :]
    own_b = slab_group == (tile_id * gpt) % gps + col_group

    def btiles(bb):
        t = bb.reshape(h, half_tiles, MXU_DIM).transpose(1, 0, 2)
        t = jnp.where(own_b, t[:, None], 0.0)
        return t.reshape(half_tiles, LANES, MXU_DIM).astype(BF16)

    bt = jnp.concatenate([btiles(bb_re), btiles(bb_im)], axis=0)

    width = g * h
    gpo = MXU_DIM // h
    own_c = (jnp.arange(gpo)[None, :, None, None, None]
             == jnp.arange(gpo)[None, None, None, :, None])

    def ctiles(c):
        t = c.astype(F32).reshape(width // MXU_DIM, gpo, h, p).transpose(0, 1, 3, 2)
        t = jnp.where(own_c, t[:, :, :, None, :], 0.0)
        return t.reshape(width // MXU_DIM, gpo * p, MXU_DIM).astype(BF16)

    return bt, ctiles(c_re), ctiles(c_im), l_re.reshape(1, ns), l_im.reshape(1, ns)


def _merge_kernel(x_ref, a_ref, y_ref, wg_ref, wsb_ref, wglu_ref, wo_ref, g_ref, b_ref,
                  o_ref, *, cw):
    x = x_ref[...]
    xb = x.astype(BF16)
    attn = a_ref[...]
    y = y_ref[...]
    d = x.shape[1]
    n_chunks = d // cw

    def branch_dots(c):
        cs = slice(c * cw, (c + 1) * cw)
        gs = slice(d + c * cw, d + (c + 1) * cw)
        return (_dot(attn, wsb_ref[:, cs]), _dot(y, wglu_ref[:, cs]), _dot(y, wglu_ref[:, gs]),
                _dot(xb, wg_ref[:, cs]), _dot(xb, wg_ref[:, gs]))

    def mix(dots):
        attn_br, glu_val, glu_gate, gate_attn, gate_ssm = dots
        ssm_br = glu_val * jax.nn.sigmoid(glu_gate)
        return (jax.nn.sigmoid(gate_attn) * attn_br
                + jax.nn.sigmoid(gate_ssm) * ssm_br).astype(BF16)

    acc = DN_ALPHA * x
    dots = branch_dots(0)
    for c in range(n_chunks):
        mixed = mix(dots)
        if c + 1 < n_chunks:
            dots = branch_dots(c + 1)
        acc = acc + _dot(mixed, wo_ref[c * cw:(c + 1) * cw, :])
    o_ref[...] = _layer_norm(acc, g_ref[...], b_ref[...])


def _merge(x2, attn2, y2, w_g, w_sb, w_glu, w_o, ln_g, ln_b, tm):
    n, d = x2.shape
    sbw = attn2.shape[1]
    ssw = w_glu.shape[0]
    row = lambda i: (i, 0)
    return pl.pallas_call(
        functools.partial(_merge_kernel, cw=MXU_DIM),
        grid=(n // tm,),
        in_specs=[pl.BlockSpec((tm, d), row), pl.BlockSpec((tm, sbw), row),
                  pl.BlockSpec((tm, ssw), row),
                  _resident(w_g.shape), _resident(w_sb.shape), _resident(w_glu.shape),
                  _resident(w_o.shape), _resident(ln_g.shape), _resident(ln_b.shape)],
        out_specs=pl.BlockSpec((tm, d), row),
        out_shape=jax.ShapeDtypeStruct((n, d), F32),
        compiler_params=pltpu.CompilerParams(
            dimension_semantics=("parallel",), vmem_limit_bytes=VMEM_LIMIT),
        name="merge",
    )(x2, attn2, y2, w_g, w_sb, w_glu, w_o, ln_g, ln_b)


def _ffn_kernel(h_ref, halo_ref, p_ref, wup_ref, cw_ref, cb_ref, wdn_ref, wpe_ref, wpg_ref,
                g_ref, b_ref, o_ref, up_ref, act_ref, *, tpb, n_chunks, halo):
    h = h_ref[...]
    hb = h.astype(BF16)
    tm = h.shape[0]
    first = pl.program_id(0) % tpb == 0
    hb_ext = jnp.concatenate(
        [jnp.where(first, 0.0, halo_ref[...]).astype(BF16), hb], axis=0)

    cw = MXU_DIM
    dff = n_chunks * cw

    def up_proj(j):
        for part in range(2):
            c0 = part * dff + j * cw
            up_ref[j % 2, part] = _dot(hb_ext, wup_ref[:, c0:c0 + cw])

    def conv(slot, part, c0):
        w = cw_ref[:, c0:c0 + cw]
        return (w[2:3] * up_ref[slot, part, pl.ds(halo, tm), :]
                + w[1:2] * up_ref[slot, part, pl.ds(halo - 1, tm), :]
                + (w[0:1] * up_ref[slot, part, pl.ds(halo - 2, tm), :] + cb_ref[:, c0:c0 + cw]))

    def activation(j):
        val = conv(j % 2, 0, j * cw)
        gate = conv(j % 2, 1, dff + j * cw)
        act_ref[j] = (gate * jax.nn.sigmoid(gate) * val).astype(BF16)

    up_proj(0)
    res = DN_ALPHA * h + _dot(p_ref[...].astype(BF16), wpe_ref[...]) * jax.nn.sigmoid(
        _dot(hb, wpg_ref[...]))

    for j in range(n_chunks - 1):
        up_proj(j + 1)
        activation(j)
    activation(n_chunks - 1)
    for j in range(n_chunks):
        res = res + _dot(act_ref[j], wdn_ref[j * cw:(j + 1) * cw, :])
    o_ref[...] = _layer_norm(res, g_ref[...], b_ref[...])


def _ffn(h1, p2, w_up_t, conv_w_t, conv_b_t, w_dn_t, w_pe, w_pg, ln_g, ln_b, t, tm):
    n, d = h1.shape
    pd = p2.shape[1]
    halo = BF16_SUBLANES
    tpb = t // tm
    n_chunks = w_dn_t.shape[0] // MXU_DIM
    row = lambda i: (i, 0)
    return pl.pallas_call(
        functools.partial(_ffn_kernel, tpb=tpb, n_chunks=n_chunks, halo=halo),
        grid=(n // tm,),
        in_specs=[pl.BlockSpec((tm, d), row),
                  pl.BlockSpec((halo, d), lambda i: (jnp.maximum(i * (tm // halo) - 1, 0), 0)),
                  pl.BlockSpec((tm, pd), row),
                  _resident(w_up_t.shape), _resident(conv_w_t.shape), _resident(conv_b_t.shape),
                  _resident(w_dn_t.shape), _resident(w_pe.shape), _resident(w_pg.shape),
                  _resident(ln_g.shape), _resident(ln_b.shape)],
        out_specs=pl.BlockSpec((tm, d), row),
        out_shape=jax.ShapeDtypeStruct((n, d), F32),
        scratch_shapes=[pltpu.VMEM((2, 2, tm + halo, MXU_DIM), F32),
                        pltpu.VMEM((n_chunks, tm, MXU_DIM), BF16)],
        compiler_params=pltpu.CompilerParams(
            dimension_semantics=("parallel",), vmem_limit_bytes=VMEM_LIMIT),
        name="ffn",
    )(h1, h1, p2, w_up_t, conv_w_t, conv_b_t, w_dn_t, w_pe, w_pg, ln_g, ln_b)


def kernel(x, p, w_in, w_sb_out, ssm_a_re, ssm_a_im, ssm_log_step, ssm_b_re, ssm_b_im,
           ssm_c_re, ssm_c_im, ssm_d, w_glu, w_o, ln1_g, ln1_b, w_up, conv_w, conv_b,
           w_down, w_pe, w_pe_gate, ln2_g, ln2_b):
    b, t, d = x.shape
    assert w_in.shape[0] == DEPTH
    sbw = N_SB_HEADS * SB_HEAD_DIM
    ssw = w_glu.shape[1]
    dff = w_down.shape[1]
    n = b * t
    tm = min(512, t)
    tq = min(MXU_DIM, t)
    tt = min(64, t)
    assert t % tm == 0 and t % tq == 0 and t % tt == 0 and dff % MXU_DIM == 0
    n_chunks = dff // MXU_DIM

    h = x.reshape(n, d)
    for i in range(DEPTH):
        wi = w_in[i].astype(BF16)
        w_qkvu, w_gates = wi[:, :3 * sbw + ssw], wi[:, 3 * sbw + ssw:]

        q, k, v, u = _proj(h, w_qkvu, tm, sbw, ssw)
        attn = _attention(q.reshape(b, t, sbw), k.reshape(b, t, sbw), v.reshape(b, t, sbw), tq,
                          ATTN_HEADS_PER_STEP)

        l_re, l_im, bb_re, bb_im = _ssm_prep(ssm_a_re[i], ssm_a_im[i], ssm_log_step[i],
                                             ssm_b_re[i], ssm_b_im[i])
        bt, cre, cim, lre, lim = _ssm_weights(l_re, l_im, bb_re, bb_im, ssm_c_re[i], ssm_c_im[i])
        y = _ssm(u.reshape(b, t, ssw), bt, cre, cim, lre, lim,
                 ssm_d[i].reshape(1, ssw).astype(F32), tt)

        h = _merge(h, attn.reshape(n, sbw), y.reshape(n, ssw), w_gates,
                   w_sb_out[i].astype(BF16), w_glu[i].astype(BF16), w_o[i].astype(BF16),
                   ln1_g[i].reshape(1, d), ln1_b[i].reshape(1, d), tm)

        h = _ffn(h, p[i].reshape(n, -1), w_up[i].astype(BF16), conv_w[i],
                 conv_b[i].reshape(1, -1), w_down[i].astype(BF16),
                 w_pe[i].astype(BF16), w_pe_gate[i].astype(BF16),
                 ln2_g[i].reshape(1, d), ln2_b[i].reshape(1, d), t, tm)
    return h.reshape(b, t, d)
```

```python
import functools
import math

import jax
import jax.numpy as jnp
from jax import lax
from jax.experimental import pallas as pl
from jax.experimental.pallas import tpu as pltpu

F32 = jnp.float32
BF16 = jnp.bfloat16

N_SB_HEADS = 8
SB_HEAD_DIM = 64
CONV_WIDTH = 3
DEPTH = 1
DN_ALPHA = (2.0 * DEPTH) ** 0.25
LN_EPS = 1e-5
LOG2E = 1.0 / math.log(2.0)
EXP_UNDERFLOW = 150.0 * math.log(2.0) + 1.0

LANES = 128
MXU_DIM = 256
BF16_SUBLANES = 16

VMEM_LIMIT = 56 * 1024 * 1024
ATTN_HEADS_PER_STEP = 8


def _dot(a, b):
    return jnp.dot(a, b, preferred_element_type=F32)


def _dot_nt(a, b):
    return lax.dot_general(a, b, (((1,), (1,)), ((), ())), preferred_element_type=F32)


def _layer_norm(h, g, b):
    mu = jnp.mean(h, axis=-1, keepdims=True)
    d = h - mu
    var = jnp.mean(d * d, axis=-1, keepdims=True)
    return d * lax.rsqrt(var + LN_EPS) * g + b


def _resident(shape):
    nd = len(shape)
    return pl.BlockSpec(shape, lambda *_: (0,) * nd, pipeline_mode=pl.Buffered(1))


def _ssm_prep_kernel(are_ref, aim_ref, ls_ref, bre_ref, bim_ref,
                     lre_ref, lim_ref, bbre_ref, bbim_ref, *, h):
    a_re, a_im = are_ref[...], aim_ref[...]
    step = jnp.exp(ls_ref[...])
    mag = jnp.exp(a_re * step)
    l_re = mag * jnp.cos(a_im * step)
    l_im = mag * jnp.sin(a_im * step)
    n_re, n_im = l_re - 1.0, l_im
    den = a_re * a_re + a_im * a_im
    c_re = jnp.tile((n_re * a_re + n_im * a_im) / den, (h, 1))
    c_im = jnp.tile((n_im * a_re - n_re * a_im) / den, (h, 1))
    b_re, b_im = bre_ref[...], bim_ref[...]
    lre_ref[...] = l_re
    lim_ref[...] = l_im
    bbre_ref[...] = c_re * b_re - c_im * b_im
    bbim_ref[...] = c_re * b_im + c_im * b_re


def _ssm_prep(a_re, a_im, log_step, b_re, b_im):
    g, p = a_re.shape
    h = b_re.shape[-1]
    chan_major = lambda b: b.astype(F32).transpose(2, 0, 1).reshape(h * g, p)
    l_re, l_im, bb_re, bb_im = pl.pallas_call(
        functools.partial(_ssm_prep_kernel, h=h),
        out_shape=[jax.ShapeDtypeStruct((g, p), F32)] * 2
        + [jax.ShapeDtypeStruct((h * g, p), F32)] * 2,
        name="ssm_prep",
    )(a_re.astype(F32), a_im.astype(F32), log_step.reshape(g, 1).astype(F32),
      chan_major(b_re), chan_major(b_im))
    return l_re, l_im, bb_re.reshape(h, g, p), bb_im.reshape(h, g, p)


def _proj_kernel(x_ref, w_ref, q_ref, k_ref, v_ref, u_ref, *, sbw, scale):
    xb = x_ref[...].astype(BF16)
    q_ref[...] = (_dot(xb, w_ref[:, 0:sbw]) * scale).astype(BF16)
    k_ref[...] = _dot(xb, w_ref[:, sbw:2 * sbw]).astype(BF16)
    v_ref[...] = _dot(xb, w_ref[:, 2 * sbw:3 * sbw]).astype(BF16)
    u_ref[...] = _dot(xb, w_ref[:, 3 * sbw:])


def _proj(x2, w_qkvu, tm, sbw, ssw):
    n, d = x2.shape
    row = lambda i: (i, 0)
    return pl.pallas_call(
        functools.partial(_proj_kernel, sbw=sbw, scale=1.0 / math.sqrt(SB_HEAD_DIM)),
        grid=(n // tm,),
        in_specs=[pl.BlockSpec((tm, d), row), _resident(w_qkvu.shape)],
        out_specs=[pl.BlockSpec((tm, sbw), row)] * 3 + [pl.BlockSpec((tm, ssw), row)],
        out_shape=[jax.ShapeDtypeStruct((n, sbw), BF16)] * 3
        + [jax.ShapeDtypeStruct((n, ssw), F32)],
        compiler_params=pltpu.CompilerParams(
            dimension_semantics=("parallel",), vmem_limit_bytes=VMEM_LIMIT),
        name="proj",
    )(x2, w_qkvu)


def _attn_kernel(q_ref, k_ref, v_ref, tri_ref, o_ref, z_ref, acc_ref, c_ref, *, tq, heads):
    qi = pl.program_id(2)
    neg_tri = tri_ref[...]
    rows = lax.broadcasted_iota(jnp.int32, (tq, tq), 0)
    cols = lax.broadcasted_iota(jnp.int32, (tq, tq), 1)
    causal = cols < rows
    hd = SB_HEAD_DIM

    pairs = heads // 2
    lane = lax.broadcasted_iota(jnp.int32, (tq, LANES), 1)
    low_half = lane < hd

    q_heads = []
    for pr in range(pairs):
        qp = q_ref[0, :, pr * LANES:(pr + 1) * LANES]
        q_heads += [jnp.where(low_half, qp, 0), jnp.where(low_half, 0, qp)]

    def scores(k0):
        return [_dot_nt(q_heads[h], k_ref[0, pl.ds(k0, tq), (h // 2) * LANES:(h // 2 + 1) * LANES])
                for h in range(heads)]

    def tile(k0, next_k0, mask):
        zs = [z_ref[h] for h in range(heads)]
        sps = []
        for z in zs:
            sp = jnp.maximum(z, 0.0) + jnp.log(1.0 + jnp.exp2(jnp.abs(z) * -LOG2E))
            sps.append(sp if mask is None else jnp.where(mask, sp, 0.0))
        args = [(z - sp) + _dot(sp.astype(BF16), neg_tri) for z, sp in zip(zs, sps)]
        for h, z in enumerate(scores(next_k0)):
            z_ref[h] = z
        ws = []
        for arg in args:
            w = jnp.exp(arg)
            ws.append((w if mask is None else jnp.where(mask, w, 0.0)).astype(BF16))
        for pr in range(pairs):
            vp = v_ref[0, pl.ds(k0, tq), pr * LANES:(pr + 1) * LANES]
            v2 = jnp.concatenate([jnp.where(low_half, vp, 0), jnp.where(low_half, 0, vp)], axis=0)
            pv = _dot(jnp.concatenate(ws[2 * pr:2 * pr + 2], axis=1), v2)
            sums = jnp.where(low_half, jnp.sum(sps[2 * pr], axis=-1, keepdims=True),
                             jnp.sum(sps[2 * pr + 1], axis=-1, keepdims=True))
            if mask is not None:
                acc_ref[pr] = pv
                c_ref[pr] = sums
            else:
                c = c_ref[pr]
                acc_ref[pr] += jnp.exp(-c) * pv
                c_ref[pr] = c + sums

    def block_start(j):
        return pl.multiple_of(jnp.maximum(j, 0) * tq, tq)

    for h, z in enumerate(scores(block_start(qi))):
        z_ref[h] = z
    tile(block_start(qi), block_start(qi - 1), causal)

    def min_carry():
        m = c_ref[0]
        for pr in range(1, pairs):
            m = jnp.minimum(m, c_ref[pr])
        return jnp.min(m)

    def more(state):
        i, cmin = state
        return jnp.logical_and(i < qi, cmin < EXP_UNDERFLOW)

    def body(state):
        i, _ = state
        j = qi - 1 - i
        tile(block_start(j), block_start(j - 1), None)
        return i + 1, min_carry()

    lax.while_loop(more, body, (jnp.int32(0), min_carry()))
    o_ref[0] = jnp.concatenate([acc_ref[pr] for pr in range(pairs)], axis=-1).astype(BF16)


def _attention(q, k, v, tq, heads):
    b, t, w = q.shape
    lanes = heads * SB_HEAD_DIM
    tri = -(jnp.arange(tq)[:, None] > jnp.arange(tq)[None, :]).astype(BF16)
    qspec = pl.BlockSpec((1, tq, lanes), lambda bi, hp, qi: (bi, qi, hp))
    kvspec = pl.BlockSpec((1, t, lanes), lambda bi, hp, qi: (bi, 0, hp))
    return pl.pallas_call(
        functools.partial(_attn_kernel, tq=tq, heads=heads),
        grid=(b, w // lanes, t // tq),
        in_specs=[qspec, kvspec, kvspec, pl.BlockSpec((tq, tq), lambda bi, hp, qi: (0, 0))],
        out_specs=qspec,
        out_shape=jax.ShapeDtypeStruct((b, t, w), BF16),
        scratch_shapes=[pltpu.VMEM((heads, tq, tq), F32),
                        pltpu.VMEM((heads // 2, tq, LANES), F32),
                        pltpu.VMEM((heads // 2, tq, LANES), F32)],
        compiler_params=pltpu.CompilerParams(
            dimension_semantics=("parallel", "parallel", "arbitrary"),
            vmem_limit_bytes=VMEM_LIMIT),
        name="attn",
    )(q, k, v, tri)


def _ssm_kernel(u_ref, bt_ref, cre_ref, cim_ref, lre_ref, lim_ref, d_ref, y_ref,
                s_ref, st_ref, ur_ref, yr_ref, *, tt, nb, ns, strip, pitch):
    @pl.when(pl.program_id(0) == 0)
    def _():
        st_ref[...] = jnp.zeros_like(st_ref)

    width = u_ref.shape[2]
    slabs = width // LANES
    for b in range(nb):
        for k in range(slabs):
            ur_ref[k, pl.ds(b, tt, stride=pitch), :] = u_ref[b, :, k * LANES:(k + 1) * LANES]
    u = jnp.concatenate(
        [jnp.concatenate([ur_ref[k, pitch * t:pitch * t + nb, :] for t in range(tt)], axis=0)
         for k in range(slabs)], axis=1)
    ub = u.astype(BF16)
    half_tiles = ns // MXU_DIM
    n_strips = ns // strip
    kw = ns * MXU_DIM // width

    def input_proj(s):
        for part in range(2):
            for n in range(s * strip // MXU_DIM, (s + 1) * strip // MXU_DIM):
                kt = n * MXU_DIM // (4 * LANES)
                c0 = part * ns + n * MXU_DIM
                s_ref[:, c0:c0 + MXU_DIM] = _dot(
                    ub[:, kt * LANES:(kt + 1) * LANES], bt_ref[part * half_tiles + n])

    def scan(s):
        re = slice(s * strip, (s + 1) * strip)
        im = slice(ns + s * strip, ns + (s + 1) * strip)
        lr = jnp.broadcast_to(lre_ref[:, re], (nb, strip))
        li = jnp.broadcast_to(lim_ref[:, re], (nb, strip))
        xr, xi = st_ref[:, re], st_ref[:, im]
        for t in range(tt):
            rows = slice(t * nb, (t + 1) * nb)
            xr, xi = (lr * xr - li * xi + s_ref[rows, re], lr * xi + li * xr + s_ref[rows, im])
            s_ref[rows, re] = xr
            s_ref[rows, im] = xi
        st_ref[:, re] = xr
        st_ref[:, im] = xi

    def output_proj(n):
        oc = slice(n * MXU_DIM, (n + 1) * MXU_DIM)
        xre = s_ref[:, n * kw:(n + 1) * kw].astype(BF16)
        xim = s_ref[:, ns + n * kw:ns + (n + 1) * kw].astype(BF16)
        y = jax.nn.gelu(_dot(xre, cre_ref[n]) - _dot(xim, cim_ref[n]) + d_ref[:, oc] * u[:, oc])
        for kk in range(MXU_DIM // LANES):
            for t in range(tt):
                yr_ref[n * (MXU_DIM // LANES) + kk, pitch * t:pitch * t + nb, :] = (
                    y[t * nb:(t + 1) * nb, kk * LANES:(kk + 1) * LANES])

    input_proj(0)
    for s in range(n_strips):
        if s + 1 < n_strips:
            input_proj(s + 1)
        scan(s)
        if (s + 1) * strip % kw == 0:
            output_proj((s + 1) * strip // kw - 1)
    for b in range(nb):
        for k in range(slabs):
            y_ref[b, :, k * LANES:(k + 1) * LANES] = (
                yr_ref[k, pl.ds(b, tt, stride=pitch), :].astype(BF16))


def _ssm(u3, bt, cre, cim, lre, lim, d_row, tt):
    nb, t, width = u3.shape
    ns = lre.shape[1]
    blk = tt * nb
    pitch = -(-nb // 8) * 8
    if (pitch // 8) % 2 == 0:
        pitch += 8
    block = pl.BlockSpec((nb, tt, width), lambda i: (0, i, 0))
    relayout = pltpu.VMEM((width // LANES, tt * pitch, LANES), F32)
    return pl.pallas_call(
        functools.partial(_ssm_kernel, tt=tt, nb=nb, ns=ns, strip=2 * LANES, pitch=pitch),
        grid=(t // tt,),
        in_specs=[block,
                  _resident(bt.shape), _resident(cre.shape), _resident(cim.shape),
                  _resident(lre.shape), _resident(lim.shape), _resident(d_row.shape)],
        out_specs=block,
        out_shape=jax.ShapeDtypeStruct((nb, t, width), BF16),
        scratch_shapes=[pltpu.VMEM((blk, 2 * ns), F32), pltpu.VMEM((nb, 2 * ns), F32),
                        relayout, relayout],
        compiler_params=pltpu.CompilerParams(
            dimension_semantics=("arbitrary",), vmem_limit_bytes=VMEM_LIMIT),
        name="ssm",
    )(u3, bt, cre, cim, lre, lim, d_row)


def _ssm_weights(l_re, l_im, bb_re, bb_im, c_re, c_im):
    h, g, p = bb_re.shape
    ns = g * p
    gpt = MXU_DIM // p
    gps = LANES // h
    half_tiles = ns // MXU_DIM

    tile_id = jnp.arange(half_tiles)[:, None, None, None]
    slab_group = jnp.arange(gps)[None, :, None, None]
    col_group = (jnp.arange(MXU_DIM) // p)[None, None, None, :]
    own_b = slab_group == (tile_id * gpt) % gps + col_group

    def btiles(bb):
        t = bb.reshape(h, half_tiles, MXU_DIM).transpose(1, 0, 2)
        t = jnp.where(own_b, t[:, None], 0.0)
        return t.reshape(half_tiles, LANES, MXU_DIM).astype(BF16)

    bt = jnp.concatenate([btiles(bb_re), btiles(bb_im)], axis=0)

    width = g * h
    gpo = MXU_DIM // h
    own_c = (jnp.arange(gpo)[None, :, None, None, None]
             == jnp.arange(gpo)[None, None, None, :, None])

    def ctiles(c):
        t = c.astype(F32).reshape(width // MXU_DIM, gpo, h, p).transpose(0, 1, 3, 2)
        t = jnp.where(own_c, t[:, :, :, None, :], 0.0)
        return t.reshape(width // MXU_DIM, gpo * p, MXU_DIM).astype(BF16)

    return bt, ctiles(c_re), ctiles(c_im), l_re.reshape(1, ns), l_im.reshape(1, ns)


def _merge_kernel(x_ref, a_ref, y_ref, wg_ref, wsb_ref, wglu_ref, wo_ref, g_ref, b_ref,
                  o_ref, *, cw):
    x = x_ref[...]
    xb = x.astype(BF16)
    attn = a_ref[...]
    y = y_ref[...]
    d = x.shape[1]
    n_chunks = d // cw

    def branch_dots(c):
        cs = slice(c * cw, (c + 1) * cw)
        gs = slice(d + c * cw, d + (c + 1) * cw)
        return (_dot(attn, wsb_ref[:, cs]), _dot(y, wglu_ref[:, cs]), _dot(y, wglu_ref[:, gs]),
                _dot(xb, wg_ref[:, cs]), _dot(xb, wg_ref[:, gs]))

    def mix(dots):
        attn_br, glu_val, glu_gate, gate_attn, gate_ssm = dots
        ssm_br = glu_val * jax.nn.sigmoid(glu_gate)
        return (jax.nn.sigmoid(gate_attn) * attn_br
                + jax.nn.sigmoid(gate_ssm) * ssm_br).astype(BF16)

    acc = DN_ALPHA * x
    dots = branch_dots(0)
    for c in range(n_chunks):
        mixed = mix(dots)
        if c + 1 < n_chunks:
            dots = branch_dots(c + 1)
        acc = acc + _dot(mixed, wo_ref[c * cw:(c + 1) * cw, :])
    o_ref[...] = _layer_norm(acc, g_ref[...], b_ref[...])


def _merge(x2, attn2, y2, w_g, w_sb, w_glu, w_o, ln_g, ln_b, tm):
    n, d = x2.shape
    sbw = attn2.shape[1]
    ssw = w_glu.shape[0]
    row = lambda i: (i, 0)
    return pl.pallas_call(
        functools.partial(_merge_kernel, cw=MXU_DIM),
        grid=(n // tm,),
        in_specs=[pl.BlockSpec((tm, d), row), pl.BlockSpec((tm, sbw), row),
                  pl.BlockSpec((tm, ssw), row),
                  _resident(w_g.shape), _resident(w_sb.shape), _resident(w_glu.shape),
                  _resident(w_o.shape), _resident(ln_g.shape), _resident(ln_b.shape)],
        out_specs=pl.BlockSpec((tm, d), row),
        out_shape=jax.ShapeDtypeStruct((n, d), F32),
        compiler_params=pltpu.CompilerParams(
            dimension_semantics=("parallel",), vmem_limit_bytes=VMEM_LIMIT),
        name="merge",
    )(x2, attn2, y2, w_g, w_sb, w_glu, w_o, ln_g, ln_b)


def _ffn_kernel(h_ref, halo_ref, p_ref, wup_ref, cw_ref, cb_ref, wdn_ref, wpe_ref, wpg_ref,
                g_ref, b_ref, o_ref, up_ref, act_ref, *, tpb, n_chunks, halo):
    h = h_ref[...]
    hb = h.astype(BF16)
    tm = h.shape[0]
    first = pl.program_id(0) % tpb == 0
    hb_ext = jnp.concatenate(
        [jnp.where(first, 0.0, halo_ref[...]).astype(BF16), hb], axis=0)

    cw = MXU_DIM
    dff = n_chunks * cw

    def up_proj(j):
        for part in range(2):
            c0 = part * dff + j * cw
            up_ref[j % 2, part] = _dot(hb_ext, wup_ref[:, c0:c0 + cw])

    def conv(slot, part, c0):
        w = cw_ref[:, c0:c0 + cw]
        return (w[2:3] * up_ref[slot, part, pl.ds(halo, tm), :]
                + w[1:2] * up_ref[slot, part, pl.ds(halo - 1, tm), :]
                + (w[0:1] * up_ref[slot, part, pl.ds(halo - 2, tm), :] + cb_ref[:, c0:c0 + cw]))

    def activation(j):
        val = conv(j % 2, 0, j * cw)
        gate = conv(j % 2, 1, dff + j * cw)
        act_ref[j] = (gate * jax.nn.sigmoid(gate) * val).astype(BF16)

    up_proj(0)
    res = DN_ALPHA * h + _dot(p_ref[...].astype(BF16), wpe_ref[...]) * jax.nn.sigmoid(
        _dot(hb, wpg_ref[...]))

    for j in range(n_chunks - 1):
        up_proj(j + 1)
        activation(j)
    activation(n_chunks - 1)
    for j in range(n_chunks):
        res = res + _dot(act_ref[j], wdn_ref[j * cw:(j + 1) * cw, :])
    o_ref[...] = _layer_norm(res, g_ref[...], b_ref[...])


def _ffn(h1, p2, w_up, conv_w, conv_b, w_dn, w_pe, w_pg, ln_g, ln_b, t, tm):
    n, d = h1.shape
    pd = p2.shape[1]
    halo = BF16_SUBLANES
    tpb = t // tm
    n_chunks = w_dn.shape[0] // MXU_DIM
    row = lambda i: (i, 0)
    return pl.pallas_call(
        functools.partial(_ffn_kernel, tpb=tpb, n_chunks=n_chunks, halo=halo),
        grid=(n // tm,),
        in_specs=[pl.BlockSpec((tm, d), row),
                  pl.BlockSpec((halo, d), lambda i: (jnp.maximum(i * (tm // halo) - 1, 0), 0)),
                  pl.BlockSpec((tm, pd), row),
                  _resident(w_up.shape), _resident(conv_w.shape), _resident(conv_b.shape),
                  _resident(w_dn.shape), _resident(w_pe.shape), _resident(w_pg.shape),
                  _resident(ln_g.shape), _resident(ln_b.shape)],
        out_specs=pl.BlockSpec((tm, d), row),
        out_shape=jax.ShapeDtypeStruct((n, d), F32),
        scratch_shapes=[pltpu.VMEM((2, 2, tm + halo, MXU_DIM), F32),
                        pltpu.VMEM((n_chunks, tm, MXU_DIM), BF16)],
        compiler_params=pltpu.CompilerParams(
            dimension_semantics=("parallel",), vmem_limit_bytes=VMEM_LIMIT),
        name="ffn",
    )(h1, h1, p2, w_up, conv_w, conv_b, w_dn, w_pe, w_pg, ln_g, ln_b)


def kernel(x, p, w_in, w_sb_out, ssm_a_re, ssm_a_im, ssm_log_step, ssm_b_re, ssm_b_im,
           ssm_c_re, ssm_c_im, ssm_d, w_glu, w_o, ln1_g, ln1_b, w_up, conv_w, conv_b,
           w_down, w_pe, w_pe_gate, ln2_g, ln2_b):
    b, t, d = x.shape
    assert w_in.shape[0] == DEPTH
    sbw = N_SB_HEADS * SB_HEAD_DIM
    ssw = w_glu.shape[1]
    dff = w_down.shape[1]
    n = b * t
    tm = min(512, t)
    tq = min(MXU_DIM, t)
    tt = min(64, t)
    assert t % tm == 0 and t % tq == 0 and t % tt == 0 and dff % MXU_DIM == 0

    h = x.reshape(n, d)
    for i in range(DEPTH):
        wi = w_in[i].astype(BF16)
        w_qkvu, w_gates = wi[:, :3 * sbw + ssw], wi[:, 3 * sbw + ssw:]

        q, k, v, u = _proj(h, w_qkvu, tm, sbw, ssw)
        attn = _attention(q.reshape(b, t, sbw), k.reshape(b, t, sbw), v.reshape(b, t, sbw), tq,
                          ATTN_HEADS_PER_STEP)

        l_re, l_im, bb_re, bb_im = _ssm_prep(ssm_a_re[i], ssm_a_im[i], ssm_log_step[i],
                                             ssm_b_re[i], ssm_b_im[i])
        bt, cre, cim, lre, lim = _ssm_weights(l_re, l_im, bb_re, bb_im, ssm_c_re[i], ssm_c_im[i])
        y = _ssm(u.reshape(b, t, ssw), bt, cre, cim, lre, lim,
                 ssm_d[i].reshape(1, ssw).astype(F32), tt)

        h = _merge(h, attn.reshape(n, sbw), y.reshape(n, ssw), w_gates,
                   w_sb_out[i].astype(BF16), w_glu[i].astype(BF16), w_o[i].astype(BF16),
                   ln1_g[i].reshape(1, d), ln1_b[i].reshape(1, d), min(2 * tm, t))

        h = _ffn(h, p[i].reshape(n, -1), w_up[i].astype(BF16), conv_w[i],
                 conv_b[i].reshape(1, -1), w_down[i].astype(BF16),
                 w_pe[i].astype(BF16), w_pe_gate[i].astype(BF16),
                 ln2_g[i].reshape(1, d), ln2_b[i].reshape(1, d), t, tm)
    return h.reshape(b, t, d)
```

```python
import functools
import math

import jax
import jax.numpy as jnp
from jax import lax
from jax.experimental import pallas as pl
from jax.experimental.pallas import tpu as pltpu

F32 = jnp.float32
BF16 = jnp.bfloat16

N_SB_HEADS = 8
SB_HEAD_DIM = 64
CONV_WIDTH = 3
DEPTH = 1
DN_ALPHA = (2.0 * DEPTH) ** 0.25
LN_EPS = 1e-5
LOG2E = 1.0 / math.log(2.0)
EXP_UNDERFLOW = 150.0 * math.log(2.0) + 1.0

LANES = 128
MXU_DIM = 256
BF16_SUBLANES = 16

VMEM_LIMIT = 56 * 1024 * 1024
ATTN_HEADS_PER_STEP = 8


def _dot(a, b):
    return jnp.dot(a, b, preferred_element_type=F32)


def _dot_nt(a, b):
    return lax.dot_general(a, b, (((1,), (1,)), ((), ())), preferred_element_type=F32)


def _layer_norm(h, g, b):
    mu = jnp.mean(h, axis=-1, keepdims=True)
    d = h - mu
    var = jnp.mean(d * d, axis=-1, keepdims=True)
    return d * lax.rsqrt(var + LN_EPS) * g + b


def _resident(shape):
    nd = len(shape)
    return pl.BlockSpec(shape, lambda *_: (0,) * nd, pipeline_mode=pl.Buffered(1))


def _ssm_prep_kernel(are_ref, aim_ref, ls_ref, bre_ref, bim_ref,
                     lre_ref, lim_ref, bbre_ref, bbim_ref, *, h):
    a_re, a_im = are_ref[...], aim_ref[...]
    step = jnp.exp(ls_ref[...])
    mag = jnp.exp(a_re * step)
    l_re = mag * jnp.cos(a_im * step)
    l_im = mag * jnp.sin(a_im * step)
    n_re, n_im = l_re - 1.0, l_im
    den = a_re * a_re + a_im * a_im
    c_re = jnp.tile((n_re * a_re + n_im * a_im) / den, (h, 1))
    c_im = jnp.tile((n_im * a_re - n_re * a_im) / den, (h, 1))
    b_re, b_im = bre_ref[...], bim_ref[...]
    lre_ref[...] = l_re
    lim_ref[...] = l_im
    bbre_ref[...] = c_re * b_re - c_im * b_im
    bbim_ref[...] = c_re * b_im + c_im * b_re


def _ssm_prep(a_re, a_im, log_step, b_re, b_im):
    g, p = a_re.shape
    h = b_re.shape[-1]
    chan_major = lambda b: b.astype(F32).transpose(2, 0, 1).reshape(h * g, p)
    l_re, l_im, bb_re, bb_im = pl.pallas_call(
        functools.partial(_ssm_prep_kernel, h=h),
        out_shape=[jax.ShapeDtypeStruct((g, p), F32)] * 2
        + [jax.ShapeDtypeStruct((h * g, p), F32)] * 2,
        name="ssm_prep",
    )(a_re.astype(F32), a_im.astype(F32), log_step.reshape(g, 1).astype(F32),
      chan_major(b_re), chan_major(b_im))
    return l_re, l_im, bb_re.reshape(h, g, p), bb_im.reshape(h, g, p)


def _proj_kernel(x_ref, w_ref, q_ref, k_ref, v_ref, u_ref, *, sbw, scale):
    xb = x_ref[...].astype(BF16)
    q_ref[...] = (_dot(xb, w_ref[:, 0:sbw]) * scale).astype(BF16)
    k_ref[...] = _dot(xb, w_ref[:, sbw:2 * sbw]).astype(BF16)
    v_ref[...] = _dot(xb, w_ref[:, 2 * sbw:3 * sbw]).astype(BF16)
    u_ref[...] = _dot(xb, w_ref[:, 3 * sbw:])


def _proj(x2, w_qkvu, tm, sbw, ssw):
    n, d = x2.shape
    row = lambda i: (i, 0)
    return pl.pallas_call(
        functools.partial(_proj_kernel, sbw=sbw, scale=1.0 / math.sqrt(SB_HEAD_DIM)),
        grid=(n // tm,),
        in_specs=[pl.BlockSpec((tm, d), row), _resident(w_qkvu.shape)],
        out_specs=[pl.BlockSpec((tm, sbw), row)] * 3 + [pl.BlockSpec((tm, ssw), row)],
        out_shape=[jax.ShapeDtypeStruct((n, sbw), BF16)] * 3
        + [jax.ShapeDtypeStruct((n, ssw), F32)],
        compiler_params=pltpu.CompilerParams(
            dimension_semantics=("parallel",), vmem_limit_bytes=VMEM_LIMIT),
        name="proj",
    )(x2, w_qkvu)


def _attn_kernel(q_ref, k_ref, v_ref, tri_ref, o_ref, z_ref, acc_ref, c_ref, *, tq, heads):
    qi = pl.program_id(2)
    neg_tri = tri_ref[...]
    rows = lax.broadcasted_iota(jnp.int32, (tq, tq), 0)
    cols = lax.broadcasted_iota(jnp.int32, (tq, tq), 1)
    causal = cols < rows
    hd = SB_HEAD_DIM

    pairs = heads // 2
    lane = lax.broadcasted_iota(jnp.int32, (tq, LANES), 1)
    low_half = lane < hd

    q_heads = []
    for pr in range(pairs):
        qp = q_ref[0, :, pr * LANES:(pr + 1) * LANES]
        q_heads += [jnp.where(low_half, qp, 0), jnp.where(low_half, 0, qp)]

    def scores(k0):
        return [_dot_nt(q_heads[h], k_ref[0, pl.ds(k0, tq), (h // 2) * LANES:(h // 2 + 1) * LANES])
                for h in range(heads)]

    def tile(k0, next_k0, mask):
        zs = [z_ref[h] for h in range(heads)]
        sps = []
        for z in zs:
            sp = jnp.maximum(z, 0.0) + jnp.log(1.0 + jnp.exp2(jnp.abs(z) * -LOG2E))
            sps.append(sp if mask is None else jnp.where(mask, sp, 0.0))
        args = [(z - sp) + _dot(sp.astype(BF16), neg_tri) for z, sp in zip(zs, sps)]
        if next_k0 is not None:
            for h, z in enumerate(scores(next_k0)):
                z_ref[h] = z
        ws = []
        for arg in args:
            w = jnp.exp(arg)
            ws.append((w if mask is None else jnp.where(mask, w, 0.0)).astype(BF16))
        for pr in range(pairs):
            vp = v_ref[0, pl.ds(k0, tq), pr * LANES:(pr + 1) * LANES]
            v2 = jnp.concatenate([jnp.where(low_half, vp, 0), jnp.where(low_half, 0, vp)], axis=0)
            pv = _dot(jnp.concatenate(ws[2 * pr:2 * pr + 2], axis=1), v2)
            sums = jnp.where(low_half, jnp.sum(sps[2 * pr], axis=-1, keepdims=True),
                             jnp.sum(sps[2 * pr + 1], axis=-1, keepdims=True))
            if mask is not None:
                acc_ref[pr] = pv
                c_ref[pr] = sums
            else:
                c = c_ref[pr]
                acc_ref[pr] += jnp.exp(-c) * pv
                c_ref[pr] = c + sums

    def block_start(j):
        return pl.multiple_of(jnp.maximum(j, 0) * tq, tq)

    for h, z in enumerate(scores(block_start(qi))):
        z_ref[h] = z
    tile(block_start(qi), block_start(qi - 1), causal)

    def min_carry():
        m = c_ref[0]
        for pr in range(1, pairs):
            m = jnp.minimum(m, c_ref[pr])
        return jnp.min(m)

    def more(state):
        i, cmin = state
        return jnp.logical_and(i < qi, cmin < EXP_UNDERFLOW)

    def body(state):
        i, _ = state
        j = qi - 1 - i

        @pl.when(i > 0)
        def _():
            for h, z in enumerate(scores(block_start(j))):
                z_ref[h] = z

        tile(block_start(j), None, None)
        return i + 1, min_carry()

    lax.while_loop(more, body, (jnp.int32(0), jnp.float32(0.0)))
    o_ref[0] = jnp.concatenate([acc_ref[pr] for pr in range(pairs)], axis=-1).astype(BF16)


def _attention(q, k, v, tq, heads):
    b, t, w = q.shape
    lanes = heads * SB_HEAD_DIM
    tri = -(jnp.arange(tq)[:, None] > jnp.arange(tq)[None, :]).astype(BF16)
    qspec = pl.BlockSpec((1, tq, lanes), lambda bi, hp, qi: (bi, qi, hp))
    kvspec = pl.BlockSpec((1, t, lanes), lambda bi, hp, qi: (bi, 0, hp))
    return pl.pallas_call(
        functools.partial(_attn_kernel, tq=tq, heads=heads),
        grid=(b, w // lanes, t // tq),
        in_specs=[qspec, kvspec, kvspec, pl.BlockSpec((tq, tq), lambda bi, hp, qi: (0, 0))],
        out_specs=qspec,
        out_shape=jax.ShapeDtypeStruct((b, t, w), BF16),
        scratch_shapes=[pltpu.VMEM((heads, tq, tq), F32),
                        pltpu.VMEM((heads // 2, tq, LANES), F32),
                        pltpu.VMEM((heads // 2, tq, LANES), F32)],
        compiler_params=pltpu.CompilerParams(
            dimension_semantics=("parallel", "parallel", "arbitrary"),
            vmem_limit_bytes=VMEM_LIMIT),
        name="attn",
    )(q, k, v, tri)


def _ssm_kernel(u_ref, bt_ref, cre_ref, cim_ref, lre_ref, lim_ref, d_ref, y_ref,
                s_ref, st_ref, ur_ref, yr_ref, *, tt, nb, ns, strip, pitch):
    @pl.when(pl.program_id(0) == 0)
    def _():
        st_ref[...] = jnp.zeros_like(st_ref)

    width = u_ref.shape[2]
    slabs = width // LANES
    for b in range(nb):
        for k in range(slabs):
            ur_ref[k, pl.ds(b, tt, stride=pitch), :] = u_ref[b, :, k * LANES:(k + 1) * LANES]
    u = jnp.concatenate(
        [jnp.concatenate([ur_ref[k, pitch * t:pitch * t + nb, :] for t in range(tt)], axis=0)
         for k in range(slabs)], axis=1)
    ub = u.astype(BF16)
    half_tiles = ns // MXU_DIM
    n_strips = ns // strip
    kw = ns * MXU_DIM // width

    def input_proj(s):
        for part in range(2):
            for n in range(s * strip // MXU_DIM, (s + 1) * strip // MXU_DIM):
                kt = n * MXU_DIM // (4 * LANES)
                c0 = part * ns + n * MXU_DIM
                s_ref[:, c0:c0 + MXU_DIM] = _dot(
                    ub[:, kt * LANES:(kt + 1) * LANES], bt_ref[part * half_tiles + n])

    def scan(s):
        re = slice(s * strip, (s + 1) * strip)
        im = slice(ns + s * strip, ns + (s + 1) * strip)
        lr = jnp.broadcast_to(lre_ref[:, re], (nb, strip))
        li = jnp.broadcast_to(lim_ref[:, re], (nb, strip))
        xr, xi = st_ref[:, re], st_ref[:, im]
        for t in range(tt):
            rows = slice(t * nb, (t + 1) * nb)
            xr, xi = (lr * xr - li * xi + s_ref[rows, re], lr * xi + li * xr + s_ref[rows, im])
            s_ref[rows, re] = xr
            s_ref[rows, im] = xi
        st_ref[:, re] = xr
        st_ref[:, im] = xi

    def output_proj(n):
        oc = slice(n * MXU_DIM, (n + 1) * MXU_DIM)
        xre = s_ref[:, n * kw:(n + 1) * kw].astype(BF16)
        xim = s_ref[:, ns + n * kw:ns + (n + 1) * kw].astype(BF16)
        y = jax.nn.gelu(_dot(xre, cre_ref[n]) - _dot(xim, cim_ref[n]) + d_ref[:, oc] * u[:, oc])
        for kk in range(MXU_DIM // LANES):
            for t in range(tt):
                yr_ref[n * (MXU_DIM // LANES) + kk, pitch * t:pitch * t + nb, :] = (
                    y[t * nb:(t + 1) * nb, kk * LANES:(kk + 1) * LANES])

    input_proj(0)
    for s in range(n_strips):
        if s + 1 < n_strips:
            input_proj(s + 1)
        scan(s)
        if (s + 1) * strip % kw == 0:
            output_proj((s + 1) * strip // kw - 1)
    for b in range(nb):
        for k in range(slabs):
            y_ref[b, :, k * LANES:(k + 1) * LANES] = (
                yr_ref[k, pl.ds(b, tt, stride=pitch), :].astype(BF16))


def _ssm(u3, bt, cre, cim, lre, lim, d_row, tt):
    nb, t, width = u3.shape
    ns = lre.shape[1]
    blk = tt * nb
    pitch = -(-nb // 8) * 8
    if (pitch // 8) % 2 == 0:
        pitch += 8
    block = pl.BlockSpec((nb, tt, width), lambda i: (0, i, 0))
    relayout = pltpu.VMEM((width // LANES, tt * pitch, LANES), F32)
    return pl.pallas_call(
        functools.partial(_ssm_kernel, tt=tt, nb=nb, ns=ns, strip=2 * LANES, pitch=pitch),
        grid=(t // tt,),
        in_specs=[block,
                  _resident(bt.shape), _resident(cre.shape), _resident(cim.shape),
                  _resident(lre.shape), _resident(lim.shape), _resident(d_row.shape)],
        out_specs=block,
        out_shape=jax.ShapeDtypeStruct((nb, t, width), BF16),
        scratch_shapes=[pltpu.VMEM((blk, 2 * ns), F32), pltpu.VMEM((nb, 2 * ns), F32),
                        relayout, relayout],
        compiler_params=pltpu.CompilerParams(
            dimension_semantics=("arbitrary",), vmem_limit_bytes=VMEM_LIMIT),
        name="ssm",
    )(u3, bt, cre, cim, lre, lim, d_row)


def _ssm_weights(l_re, l_im, bb_re, bb_im, c_re, c_im):
    h, g, p = bb_re.shape
    ns = g * p
    gpt = MXU_DIM // p
    gps = LANES // h
    half_tiles = ns // MXU_DIM

    tile_id = jnp.arange(half_tiles)[:, None, None, None]
    slab_group = jnp.arange(gps)[None, :, None, None]
    col_group = (jnp.arange(MXU_DIM) // p)[None, None, None, :]
    own_b = slab_group == (tile_id * gpt) % gps + col_group

    def btiles(bb):
        t = bb.reshape(h, half_tiles, MXU_DIM).transpose(1, 0, 2)
        t = jnp.where(own_b, t[:, None], 0.0)
        return t.reshape(half_tiles, LANES, MXU_DIM).astype(BF16)

    bt = jnp.concatenate([btiles(bb_re), btiles(bb_im)], axis=0)

    width = g * h
    gpo = MXU_DIM // h
    own_c = (jnp.arange(gpo)[None, :, None, None, None]
             == jnp.arange(gpo)[None, None, None, :, None])

    def ctiles(c):
        t = c.astype(F32).reshape(width // MXU_DIM, gpo, h, p).transpose(0, 1, 3, 2)
        t = jnp.where(own_c, t[:, :, :, None, :], 0.0)
        return t.reshape(width // MXU_DIM, gpo * p, MXU_DIM).astype(BF16)

    return bt, ctiles(c_re), ctiles(c_im), l_re.reshape(1, ns), l_im.reshape(1, ns)


def _merge_kernel(x_ref, a_ref, y_ref, wg_ref, wsb_ref, wglu_ref, wo_ref, g_ref, b_ref,
                  o_ref, *, cw):
    x = x_ref[...]
    xb = x.astype(BF16)
    attn = a_ref[...]
    y = y_ref[...]
    d = x.shape[1]
    n_chunks = d // cw

    def branch_dots(c):
        cs = slice(c * cw, (c + 1) * cw)
        gs = slice(d + c * cw, d + (c + 1) * cw)
        return (_dot(attn, wsb_ref[:, cs]), _dot(y, wglu_ref[:, cs]), _dot(y, wglu_ref[:, gs]),
                _dot(xb, wg_ref[:, cs]), _dot(xb, wg_ref[:, gs]))

    def mix(dots):
        attn_br, glu_val, glu_gate, gate_attn, gate_ssm = dots
        ssm_br = glu_val * jax.nn.sigmoid(glu_gate)
        return (jax.nn.sigmoid(gate_attn) * attn_br
                + jax.nn.sigmoid(gate_ssm) * ssm_br).astype(BF16)

    acc = DN_ALPHA * x
    dots = branch_dots(0)
    for c in range(n_chunks):
        mixed = mix(dots)
        if c + 1 < n_chunks:
            dots = branch_dots(c + 1)
        acc = acc + _dot(mixed, wo_ref[c * cw:(c + 1) * cw, :])
    o_ref[...] = _layer_norm(acc, g_ref[...], b_ref[...])


def _merge(x2, attn2, y2, w_g, w_sb, w_glu, w_o, ln_g, ln_b, tm):
    n, d = x2.shape
    sbw = attn2.shape[1]
    ssw = w_glu.shape[0]
    row = lambda i: (i, 0)
    return pl.pallas_call(
        functools.partial(_merge_kernel, cw=MXU_DIM),
        grid=(n // tm,),
        in_specs=[pl.BlockSpec((tm, d), row), pl.BlockSpec((tm, sbw), row),
                  pl.BlockSpec((tm, ssw), row),
                  _resident(w_g.shape), _resident(w_sb.shape), _resident(w_glu.shape),
                  _resident(w_o.shape), _resident(ln_g.shape), _resident(ln_b.shape)],
        out_specs=pl.BlockSpec((tm, d), row),
        out_shape=jax.ShapeDtypeStruct((n, d), F32),
        compiler_params=pltpu.CompilerParams(
            dimension_semantics=("parallel",), vmem_limit_bytes=VMEM_LIMIT),
        name="merge",
    )(x2, attn2, y2, w_g, w_sb, w_glu, w_o, ln_g, ln_b)


def _ffn_kernel(h_ref, halo_ref, p_ref, wup_ref, cw_ref, cb_ref, wdn_ref, wpe_ref, wpg_ref,
                g_ref, b_ref, o_ref, up_ref, act_ref, *, tpb, n_chunks, halo):
    h = h_ref[...]
    hb = h.astype(BF16)
    tm = h.shape[0]
    first = pl.program_id(0) % tpb == 0
    hb_ext = jnp.concatenate(
        [jnp.where(first, 0.0, halo_ref[...]).astype(BF16), hb], axis=0)

    cw = MXU_DIM
    dff = n_chunks * cw

    def up_proj(j):
        for part in range(2):
            c0 = part * dff + j * cw
            up_ref[j % 2, part] = _dot(hb_ext, wup_ref[:, c0:c0 + cw])

    def conv(slot, part, c0):
        w = cw_ref[:, c0:c0 + cw]
        return (w[2:3] * up_ref[slot, part, pl.ds(halo, tm), :]
                + w[1:2] * up_ref[slot, part, pl.ds(halo - 1, tm), :]
                + (w[0:1] * up_ref[slot, part, pl.ds(halo - 2, tm), :] + cb_ref[:, c0:c0 + cw]))

    def activation(j):
        val = conv(j % 2, 0, j * cw)
        gate = conv(j % 2, 1, dff + j * cw)
        act_ref[j] = (gate * jax.nn.sigmoid(gate) * val).astype(BF16)

    up_proj(0)
    res = DN_ALPHA * h + _dot(p_ref[...].astype(BF16), wpe_ref[...]) * jax.nn.sigmoid(
        _dot(hb, wpg_ref[...]))

    for j in range(n_chunks - 1):
        up_proj(j + 1)
        activation(j)
    activation(n_chunks - 1)
    for j in range(n_chunks):
        res = res + _dot(act_ref[j], wdn_ref[j * cw:(j + 1) * cw, :])
    o_ref[...] = _layer_norm(res, g_ref[...], b_ref[...])


def _ffn(h1, p2, w_up, conv_w, conv_b, w_dn, w_pe, w_pg, ln_g, ln_b, t, tm):
    n, d = h1.shape
    pd = p2.shape[1]
    halo = BF16_SUBLANES
    tpb = t // tm
    n_chunks = w_dn.shape[0] // MXU_DIM
    row = lambda i: (i, 0)
    return pl.pallas_call(
        functools.partial(_ffn_kernel, tpb=tpb, n_chunks=n_chunks, halo=halo),
        grid=(n // tm,),
        in_specs=[pl.BlockSpec((tm, d), row),
                  pl.BlockSpec((halo, d), lambda i: (jnp.maximum(i * (tm // halo) - 1, 0), 0)),
                  pl.BlockSpec((tm, pd), row),
                  _resident(w_up.shape), _resident(conv_w.shape), _resident(conv_b.shape),
                  _resident(w_dn.shape), _resident(w_pe.shape), _resident(w_pg.shape),
                  _resident(ln_g.shape), _resident(ln_b.shape)],
        out_specs=pl.BlockSpec((tm, d), row),
        out_shape=jax.ShapeDtypeStruct((n, d), F32),
        scratch_shapes=[pltpu.VMEM((2, 2, tm + halo, MXU_DIM), F32),
                        pltpu.VMEM((n_chunks, tm, MXU_DIM), BF16)],
        compiler_params=pltpu.CompilerParams(
            dimension_semantics=("parallel",), vmem_limit_bytes=VMEM_LIMIT),
        name="ffn",
    )(h1, h1, p2, w_up, conv_w, conv_b, w_dn, w_pe, w_pg, ln_g, ln_b)


def kernel(x, p, w_in, w_sb_out, ssm_a_re, ssm_a_im, ssm_log_step, ssm_b_re, ssm_b_im,
           ssm_c_re, ssm_c_im, ssm_d, w_glu, w_o, ln1_g, ln1_b, w_up, conv_w, conv_b,
           w_down, w_pe, w_pe_gate, ln2_g, ln2_b):
    b, t, d = x.shape
    assert w_in.shape[0] == DEPTH
    sbw = N_SB_HEADS * SB_HEAD_DIM
    ssw = w_glu.shape[1]
    dff = w_down.shape[1]
    n = b * t
    tm = min(512, t)
    tq = min(MXU_DIM, t)
    tt = min(64, t)
    assert t % tm == 0 and t % tq == 0 and t % tt == 0 and dff % MXU_DIM == 0

    h = x.reshape(n, d)
    for i in range(DEPTH):
        wi = w_in[i].astype(BF16)
        w_qkvu, w_gates = wi[:, :3 * sbw + ssw], wi[:, 3 * sbw + ssw:]

        q, k, v, u = _proj(h, w_qkvu, min(2 * tm, t), sbw, ssw)
        attn = _attention(q.reshape(b, t, sbw), k.reshape(b, t, sbw), v.reshape(b, t, sbw), tq,
                          ATTN_HEADS_PER_STEP)

        l_re, l_im, bb_re, bb_im = _ssm_prep(ssm_a_re[i], ssm_a_im[i], ssm_log_step[i],
                                             ssm_b_re[i], ssm_b_im[i])
        bt, cre, cim, lre, lim = _ssm_weights(l_re, l_im, bb_re, bb_im, ssm_c_re[i], ssm_c_im[i])
        y = _ssm(u.reshape(b, t, ssw), bt, cre, cim, lre, lim,
                 ssm_d[i].reshape(1, ssw).astype(F32), tt)

        h = _merge(h, attn.reshape(n, sbw), y.reshape(n, ssw), w_gates,
                   w_sb_out[i].astype(BF16), w_glu[i].astype(BF16), w_o[i].astype(BF16),
                   ln1_g[i].reshape(1, d), ln1_b[i].reshape(1, d), min(2 * tm, t))

        h = _ffn(h, p[i].reshape(n, -1), w_up[i].astype(BF16), conv_w[i],
                 conv_b[i].reshape(1, -1), w_down[i].astype(BF16),
                 w_pe[i].astype(BF16), w_pe_gate[i].astype(BF16),
                 ln2_g[i].reshape(1, d), ln2_b[i].reshape(1, d), t, tm)
    return h.reshape(b, t, d)
```

```python
import functools
import math

import jax
import jax.numpy as jnp
from jax import lax
from jax.experimental import pallas as pl
from jax.experimental.pallas import tpu as pltpu

F32 = jnp.float32
BF16 = jnp.bfloat16

N_SB_HEADS = 8
SB_HEAD_DIM = 64
CONV_WIDTH = 3
DEPTH = 1
DN_ALPHA = (2.0 * DEPTH) ** 0.25
LN_EPS = 1e-5
LOG2E = 1.0 / math.log(2.0)
EXP_UNDERFLOW = 150.0 * math.log(2.0) + 1.0

LANES = 128
MXU_DIM = 256
BF16_SUBLANES = 16

VMEM_LIMIT = 56 * 1024 * 1024
ATTN_HEADS_PER_STEP = 8


def _dot(a, b):
    return jnp.dot(a, b, preferred_element_type=F32)


def _dot_nt(a, b):
    return lax.dot_general(a, b, (((1,), (1,)), ((), ())), preferred_element_type=F32)


def _layer_norm(h, g, b):
    mu = jnp.mean(h, axis=-1, keepdims=True)
    d = h - mu
    var = jnp.mean(d * d, axis=-1, keepdims=True)
    return d * lax.rsqrt(var + LN_EPS) * g + b


def _resident(shape):
    nd = len(shape)
    return pl.BlockSpec(shape, lambda *_: (0,) * nd, pipeline_mode=pl.Buffered(1))


def _ssm_prep_kernel(are_ref, aim_ref, ls_ref, bre_ref, bim_ref,
                     lre_ref, lim_ref, bbre_ref, bbim_ref, *, h):
    a_re, a_im = are_ref[...], aim_ref[...]
    step = jnp.exp(ls_ref[...])
    mag = jnp.exp(a_re * step)
    l_re = mag * jnp.cos(a_im * step)
    l_im = mag * jnp.sin(a_im * step)
    n_re, n_im = l_re - 1.0, l_im
    den = a_re * a_re + a_im * a_im
    c_re = jnp.tile((n_re * a_re + n_im * a_im) / den, (h, 1))
    c_im = jnp.tile((n_im * a_re - n_re * a_im) / den, (h, 1))
    b_re, b_im = bre_ref[...], bim_ref[...]
    lre_ref[...] = l_re
    lim_ref[...] = l_im
    bbre_ref[...] = c_re * b_re - c_im * b_im
    bbim_ref[...] = c_re * b_im + c_im * b_re


def _ssm_prep(a_re, a_im, log_step, b_re, b_im):
    g, p = a_re.shape
    h = b_re.shape[-1]
    chan_major = lambda b: b.astype(F32).transpose(2, 0, 1).reshape(h * g, p)
    l_re, l_im, bb_re, bb_im = pl.pallas_call(
        functools.partial(_ssm_prep_kernel, h=h),
        out_shape=[jax.ShapeDtypeStruct((g, p), F32)] * 2
        + [jax.ShapeDtypeStruct((h * g, p), F32)] * 2,
        name="ssm_prep",
    )(a_re.astype(F32), a_im.astype(F32), log_step.reshape(g, 1).astype(F32),
      chan_major(b_re), chan_major(b_im))
    return l_re, l_im, bb_re.reshape(h, g, p), bb_im.reshape(h, g, p)


def _proj_kernel(x_ref, w_ref, q_ref, k_ref, v_ref, u_ref, *, sbw, scale):
    xb = x_ref[...].astype(BF16)
    q_ref[...] = (_dot(xb, w_ref[:, 0:sbw]) * scale).astype(BF16)
    k_ref[...] = _dot(xb, w_ref[:, sbw:2 * sbw]).astype(BF16)
    v_ref[...] = _dot(xb, w_ref[:, 2 * sbw:3 * sbw]).astype(BF16)
    u_ref[...] = _dot(xb, w_ref[:, 3 * sbw:])


def _proj(x2, w_qkvu, tm, sbw, ssw):
    n, d = x2.shape
    row = lambda i: (i, 0)
    return pl.pallas_call(
        functools.partial(_proj_kernel, sbw=sbw, scale=1.0 / math.sqrt(SB_HEAD_DIM)),
        grid=(n // tm,),
        in_specs=[pl.BlockSpec((tm, d), row), _resident(w_qkvu.shape)],
        out_specs=[pl.BlockSpec((tm, sbw), row)] * 3 + [pl.BlockSpec((tm, ssw), row)],
        out_shape=[jax.ShapeDtypeStruct((n, sbw), BF16)] * 3
        + [jax.ShapeDtypeStruct((n, ssw), F32)],
        compiler_params=pltpu.CompilerParams(
            dimension_semantics=("parallel",), vmem_limit_bytes=VMEM_LIMIT),
        name="proj",
    )(x2, w_qkvu)


def _attn_kernel(q_ref, k_ref, v_ref, tri_ref, o_ref, acc_ref, c_ref, *, tq, heads):
    qi = pl.program_id(2)
    neg_tri = tri_ref[...]
    rows = lax.broadcasted_iota(jnp.int32, (tq, tq), 0)
    cols = lax.broadcasted_iota(jnp.int32, (tq, tq), 1)
    causal = cols < rows
    hd = SB_HEAD_DIM

    pairs = heads // 2
    lane = lax.broadcasted_iota(jnp.int32, (tq, LANES), 1)
    low_half = lane < hd

    q_heads = []
    for pr in range(pairs):
        qp = q_ref[0, :, pr * LANES:(pr + 1) * LANES]
        q_heads += [jnp.where(low_half, qp, 0), jnp.where(low_half, 0, qp)]

    def scores(k0):
        return [_dot_nt(q_heads[h], k_ref[0, pl.ds(k0, tq), (h // 2) * LANES:(h // 2 + 1) * LANES])
                for h in range(heads)]

    def tile(k0, mask):
        zs = scores(k0)
        sps = []
        for z in zs:
            sp = jnp.maximum(z, 0.0) + jnp.log(1.0 + jnp.exp2(jnp.abs(z) * -LOG2E))
            sps.append(sp if mask is None else jnp.where(mask, sp, 0.0))
        args = [(z - sp) + _dot(sp.astype(BF16), neg_tri) for z, sp in zip(zs, sps)]
        ws = []
        for arg in args:
            w = jnp.exp(arg)
            ws.append((w if mask is None else jnp.where(mask, w, 0.0)).astype(BF16))
        for pr in range(pairs):
            vp = v_ref[0, pl.ds(k0, tq), pr * LANES:(pr + 1) * LANES]
            v2 = jnp.concatenate([jnp.where(low_half, vp, 0), jnp.where(low_half, 0, vp)], axis=0)
            pv = _dot(jnp.concatenate(ws[2 * pr:2 * pr + 2], axis=1), v2)
            sums = jnp.where(low_half, jnp.sum(sps[2 * pr], axis=-1, keepdims=True),
                             jnp.sum(sps[2 * pr + 1], axis=-1, keepdims=True))
            if mask is not None:
                acc_ref[pr] = pv
                c_ref[pr] = sums
            else:
                c = c_ref[pr]
                acc_ref[pr] += jnp.exp(-c) * pv
                c_ref[pr] = c + sums

    def block_start(j):
        return pl.multiple_of(j * tq, tq)

    tile(block_start(qi), causal)

    def min_carry():
        m = c_ref[0]
        for pr in range(1, pairs):
            m = jnp.minimum(m, c_ref[pr])
        return jnp.min(m)

    def more(state):
        i, cmin = state
        return jnp.logical_and(i < qi, cmin < EXP_UNDERFLOW)

    def body(state):
        i, _ = state
        tile(block_start(qi - 1 - i), None)
        return i + 1, min_carry()

    lax.while_loop(more, body, (jnp.int32(0), jnp.float32(0.0)))
    o_ref[0] = jnp.concatenate([acc_ref[pr] for pr in range(pairs)], axis=-1).astype(BF16)


def _attention(q, k, v, tq, heads):
    b, t, w = q.shape
    lanes = heads * SB_HEAD_DIM
    tri = -(jnp.arange(tq)[:, None] > jnp.arange(tq)[None, :]).astype(BF16)
    qspec = pl.BlockSpec((1, tq, lanes), lambda bi, hp, qi: (bi, qi, hp))
    kvspec = pl.BlockSpec((1, t, lanes), lambda bi, hp, qi: (bi, 0, hp))
    return pl.pallas_call(
        functools.partial(_attn_kernel, tq=tq, heads=heads),
        grid=(b, w // lanes, t // tq),
        in_specs=[qspec, kvspec, kvspec, pl.BlockSpec((tq, tq), lambda bi, hp, qi: (0, 0))],
        out_specs=qspec,
        out_shape=jax.ShapeDtypeStruct((b, t, w), BF16),
        scratch_shapes=[pltpu.VMEM((heads // 2, tq, LANES), F32),
                        pltpu.VMEM((heads // 2, tq, LANES), F32)],
        compiler_params=pltpu.CompilerParams(
            dimension_semantics=("parallel", "parallel", "arbitrary"),
            vmem_limit_bytes=VMEM_LIMIT),
        name="attn",
    )(q, k, v, tri)


def _ssm_kernel(u_ref, bt_ref, cre_ref, cim_ref, lre_ref, lim_ref, d_ref, y_ref,
                s_ref, st_ref, ur_ref, yr_ref, *, tt, nb, ns, strip, pitch):
    @pl.when(pl.program_id(0) == 0)
    def _():
        st_ref[...] = jnp.zeros_like(st_ref)

    width = u_ref.shape[2]
    slabs = width // LANES
    for b in range(nb):
        for k in range(slabs):
            ur_ref[k, pl.ds(b, tt, stride=pitch), :] = u_ref[b, :, k * LANES:(k + 1) * LANES]
    u = jnp.concatenate(
        [jnp.concatenate([ur_ref[k, pitch * t:pitch * t + nb, :] for t in range(tt)], axis=0)
         for k in range(slabs)], axis=1)
    ub = u.astype(BF16)
    half_tiles = ns // MXU_DIM
    n_strips = ns // strip
    kw = ns * MXU_DIM // width

    def input_proj(s):
        for part in range(2):
            for n in range(s * strip // MXU_DIM, (s + 1) * strip // MXU_DIM):
                kt = n * MXU_DIM // (4 * LANES)
                c0 = part * ns + n * MXU_DIM
                s_ref[:, c0:c0 + MXU_DIM] = _dot(
                    ub[:, kt * LANES:(kt + 1) * LANES], bt_ref[part * half_tiles + n])

    def scan(s):
        re = slice(s * strip, (s + 1) * strip)
        im = slice(ns + s * strip, ns + (s + 1) * strip)
        lr = jnp.broadcast_to(lre_ref[:, re], (nb, strip))
        li = jnp.broadcast_to(lim_ref[:, re], (nb, strip))
        xr, xi = st_ref[:, re], st_ref[:, im]
        for t in range(tt):
            rows = slice(t * nb, (t + 1) * nb)
            xr, xi = (lr * xr - li * xi + s_ref[rows, re], lr * xi + li * xr + s_ref[rows, im])
            s_ref[rows, re] = xr
            s_ref[rows, im] = xi
        st_ref[:, re] = xr
        st_ref[:, im] = xi

    def output_proj(n):
        oc = slice(n * MXU_DIM, (n + 1) * MXU_DIM)
        xre = s_ref[:, n * kw:(n + 1) * kw].astype(BF16)
        xim = s_ref[:, ns + n * kw:ns + (n + 1) * kw].astype(BF16)
        y = jax.nn.gelu(_dot(xre, cre_ref[n]) - _dot(xim, cim_ref[n]) + d_ref[:, oc] * u[:, oc])
        for kk in range(MXU_DIM // LANES):
            for t in range(tt):
                yr_ref[n * (MXU_DIM // LANES) + kk, pitch * t:pitch * t + nb, :] = (
                    y[t * nb:(t + 1) * nb, kk * LANES:(kk + 1) * LANES])

    input_proj(0)
    for s in range(n_strips):
        if s + 1 < n_strips:
            input_proj(s + 1)
        scan(s)
        if (s + 1) * strip % kw == 0:
            output_proj((s + 1) * strip // kw - 1)
    for b in range(nb):
        for k in range(slabs):
            y_ref[b, :, k * LANES:(k + 1) * LANES] = (
                yr_ref[k, pl.ds(b, tt, stride=pitch), :].astype(BF16))


def _ssm(u3, bt, cre, cim, lre, lim, d_row, tt):
    nb, t, width = u3.shape
    ns = lre.shape[1]
    blk = tt * nb
    pitch = -(-nb // 8) * 8
    if (pitch // 8) % 2 == 0:
        pitch += 8
    block = pl.BlockSpec((nb, tt, width), lambda i: (0, i, 0))
    relayout = pltpu.VMEM((width // LANES, tt * pitch, LANES), F32)
    return pl.pallas_call(
        functools.partial(_ssm_kernel, tt=tt, nb=nb, ns=ns, strip=2 * LANES, pitch=pitch),
        grid=(t // tt,),
        in_specs=[block,
                  _resident(bt.shape), _resident(cre.shape), _resident(cim.shape),
                  _resident(lre.shape), _resident(lim.shape), _resident(d_row.shape)],
        out_specs=block,
        out_shape=jax.ShapeDtypeStruct((nb, t, width), BF16),
        scratch_shapes=[pltpu.VMEM((blk, 2 * ns), F32), pltpu.VMEM((nb, 2 * ns), F32),
                        relayout, relayout],
        compiler_params=pltpu.CompilerParams(
            dimension_semantics=("arbitrary",), vmem_limit_bytes=VMEM_LIMIT),
        name="ssm",
    )(u3, bt, cre, cim, lre, lim, d_row)


def _ssm_weights(l_re, l_im, bb_re, bb_im, c_re, c_im):
    h, g, p = bb_re.shape
    ns = g * p
    gpt = MXU_DIM // p
    gps = LANES // h
    half_tiles = ns // MXU_DIM

    tile_id = jnp.arange(half_tiles)[:, None, None, None]
    slab_group = jnp.arange(gps)[None, :, None, None]
    col_group = (jnp.arange(MXU_DIM) // p)[None, None, None, :]
    own_b = slab_group == (tile_id * gpt) % gps + col_group

    def btiles(bb):
        t = bb.reshape(h, half_tiles, MXU_DIM).transpose(1, 0, 2)
        t = jnp.where(own_b, t[:, None], 0.0)
        return t.reshape(half_tiles, LANES, MXU_DIM).astype(BF16)

    bt = jnp.concatenate([btiles(bb_re), btiles(bb_im)], axis=0)

    width = g * h
    gpo = MXU_DIM // h
    own_c = (jnp.arange(gpo)[None, :, None, None, None]
             == jnp.arange(gpo)[None, None, None, :, None])

    def ctiles(c):
        t = c.astype(F32).reshape(width // MXU_DIM, gpo, h, p).transpose(0, 1, 3, 2)
        t = jnp.where(own_c, t[:, :, :, None, :], 0.0)
        return t.reshape(width // MXU_DIM, gpo * p, MXU_DIM).astype(BF16)

    return bt, ctiles(c_re), ctiles(c_im), l_re.reshape(1, ns), l_im.reshape(1, ns)


def _merge_kernel(x_ref, a_ref, y_ref, wg_ref, wsb_ref, wglu_ref, wo_ref, g_ref, b_ref,
                  o_ref, *, cw):
    x = x_ref[...]
    xb = x.astype(BF16)
    attn = a_ref[...]
    y = y_ref[...]
    d = x.shape[1]
    n_chunks = d // cw

    def branch_dots(c):
        cs = slice(c * cw, (c + 1) * cw)
        gs = slice(d + c * cw, d + (c + 1) * cw)
        return (_dot(attn, wsb_ref[:, cs]), _dot(y, wglu_ref[:, cs]), _dot(y, wglu_ref[:, gs]),
                _dot(xb, wg_ref[:, cs]), _dot(xb, wg_ref[:, gs]))

    def mix(dots):
        attn_br, glu_val, glu_gate, gate_attn, gate_ssm = dots
        ssm_br = glu_val * jax.nn.sigmoid(glu_gate)
        return (jax.nn.sigmoid(gate_attn) * attn_br
                + jax.nn.sigmoid(gate_ssm) * ssm_br).astype(BF16)

    acc = DN_ALPHA * x
    dots = branch_dots(0)
    for c in range(n_chunks):
        mixed = mix(dots)
        if c + 1 < n_chunks:
            dots = branch_dots(c + 1)
        acc = acc + _dot(mixed, wo_ref[c * cw:(c + 1) * cw, :])
    o_ref[...] = _layer_norm(acc, g_ref[...], b_ref[...])


def _merge(x2, attn2, y2, w_g, w_sb, w_glu, w_o, ln_g, ln_b, tm):
    n, d = x2.shape
    sbw = attn2.shape[1]
    ssw = w_glu.shape[0]
    row = lambda i: (i, 0)
    return pl.pallas_call(
        functools.partial(_merge_kernel, cw=MXU_DIM),
        grid=(n // tm,),
        in_specs=[pl.BlockSpec((tm, d), row), pl.BlockSpec((tm, sbw), row),
                  pl.BlockSpec((tm, ssw), row),
                  _resident(w_g.shape), _resident(w_sb.shape), _resident(w_glu.shape),
                  _resident(w_o.shape), _resident(ln_g.shape), _resident(ln_b.shape)],
        out_specs=pl.BlockSpec((tm, d), row),
        out_shape=jax.ShapeDtypeStruct((n, d), F32),
        compiler_params=pltpu.CompilerParams(
            dimension_semantics=("parallel",), vmem_limit_bytes=VMEM_LIMIT),
        name="merge",
    )(x2, attn2, y2, w_g, w_sb, w_glu, w_o, ln_g, ln_b)


def _ffn_kernel(h_ref, halo_ref, p_ref, wup_ref, cw_ref, cb_ref, wdn_ref, wpe_ref, wpg_ref,
                g_ref, b_ref, o_ref, up_ref, act_ref, *, tpb, n_chunks, halo):
    h = h_ref[...]
    hb = h.astype(BF16)
    tm = h.shape[0]
    first = pl.program_id(0) % tpb == 0
    hb_ext = jnp.concatenate(
        [jnp.where(first, 0.0, halo_ref[...]).astype(BF16), hb], axis=0)

    cw = MXU_DIM
    dff = n_chunks * cw

    def up_proj(j):
        for part in range(2):
            c0 = part * dff + j * cw
            up_ref[j % 2, part] = _dot(hb_ext, wup_ref[:, c0:c0 + cw])

    def conv(slot, part, c0):
        w = cw_ref[:, c0:c0 + cw]
        return (w[2:3] * up_ref[slot, part, pl.ds(halo, tm), :]
                + w[1:2] * up_ref[slot, part, pl.ds(halo - 1, tm), :]
                + (w[0:1] * up_ref[slot, part, pl.ds(halo - 2, tm), :] + cb_ref[:, c0:c0 + cw]))

    def activation(j):
        val = conv(j % 2, 0, j * cw)
        gate = conv(j % 2, 1, dff + j * cw)
        act_ref[j] = (gate * jax.nn.sigmoid(gate) * val).astype(BF16)

    up_proj(0)
    res = DN_ALPHA * h + _dot(p_ref[...].astype(BF16), wpe_ref[...]) * jax.nn.sigmoid(
        _dot(hb, wpg_ref[...]))

    for j in range(n_chunks - 1):
        up_proj(j + 1)
        activation(j)
    activation(n_chunks - 1)
    for j in range(n_chunks):
        res = res + _dot(act_ref[j], wdn_ref[j * cw:(j + 1) * cw, :])
    o_ref[...] = _layer_norm(res, g_ref[...], b_ref[...])


def _ffn(h1, p2, w_up, conv_w, conv_b, w_dn, w_pe, w_pg, ln_g, ln_b, t, tm):
    n, d = h1.shape
    pd = p2.shape[1]
    halo = BF16_SUBLANES
    tpb = t // tm
    n_chunks = w_dn.shape[0] // MXU_DIM
    row = lambda i: (i, 0)
    return pl.pallas_call(
        functools.partial(_ffn_kernel, tpb=tpb, n_chunks=n_chunks, halo=halo),
        grid=(n // tm,),
        in_specs=[pl.BlockSpec((tm, d), row),
                  pl.BlockSpec((halo, d), lambda i: (jnp.maximum(i * (tm // halo) - 1, 0), 0)),
                  pl.BlockSpec((tm, pd), row),
                  _resident(w_up.shape), _resident(conv_w.shape), _resident(conv_b.shape),
                  _resident(w_dn.shape), _resident(w_pe.shape), _resident(w_pg.shape),
                  _resident(ln_g.shape), _resident(ln_b.shape)],
        out_specs=pl.BlockSpec((tm, d), row),
        out_shape=jax.ShapeDtypeStruct((n, d), F32),
        scratch_shapes=[pltpu.VMEM((2, 2, tm + halo, MXU_DIM), F32),
                        pltpu.VMEM((n_chunks, tm, MXU_DIM), BF16)],
        compiler_params=pltpu.CompilerParams(
            dimension_semantics=("parallel",), vmem_limit_bytes=VMEM_LIMIT),
        name="ffn",
    )(h1, h1, p2, w_up, conv_w, conv_b, w_dn, w_pe, w_pg, ln_g, ln_b)


def kernel(x, p, w_in, w_sb_out, ssm_a_re, ssm_a_im, ssm_log_step, ssm_b_re, ssm_b_im,
           ssm_c_re, ssm_c_im, ssm_d, w_glu, w_o, ln1_g, ln1_b, w_up, conv_w, conv_b,
           w_down, w_pe, w_pe_gate, ln2_g, ln2_b):
    b, t, d = x.shape
    assert w_in.shape[0] == DEPTH
    sbw = N_SB_HEADS * SB_HEAD_DIM
    ssw = w_glu.shape[1]
    dff = w_down.shape[1]
    n = b * t
    tm = min(512, t)
    tq = min(MXU_DIM, t)
    tt = min(64, t)
    assert t % tm == 0 and t % tq == 0 and t % tt == 0 and dff % MXU_DIM == 0

    h = x.reshape(n, d)
    for i in range(DEPTH):
        wi = w_in[i].astype(BF16)
        w_qkvu, w_gates = wi[:, :3 * sbw + ssw], wi[:, 3 * sbw + ssw:]

        q, k, v, u = _proj(h, w_qkvu, min(2 * tm, t), sbw, ssw)
        attn = _attention(q.reshape(b, t, sbw), k.reshape(b, t, sbw), v.reshape(b, t, sbw), tq,
                          ATTN_HEADS_PER_STEP)

        l_re, l_im, bb_re, bb_im = _ssm_prep(ssm_a_re[i], ssm_a_im[i], ssm_log_step[i],
                                             ssm_b_re[i], ssm_b_im[i])
        bt, cre, cim, lre, lim = _ssm_weights(l_re, l_im, bb_re, bb_im, ssm_c_re[i], ssm_c_im[i])
        y = _ssm(u.reshape(b, t, ssw), bt, cre, cim, lre, lim,
                 ssm_d[i].reshape(1, ssw).astype(F32), tt)

        h = _merge(h, attn.reshape(n, sbw), y.reshape(n, ssw), w_gates,
                   w_sb_out[i].astype(BF16), w_glu[i].astype(BF16), w_o[i].astype(BF16),
                   ln1_g[i].reshape(1, d), ln1_b[i].reshape(1, d), min(2 * tm, t))

        h = _ffn(h, p[i].reshape(n, -1), w_up[i].astype(BF16), conv_w[i],
                 conv_b[i].reshape(1, -1), w_down[i].astype(BF16),
                 w_pe[i].astype(BF16), w_pe_gate[i].astype(BF16),
                 ln2_g[i].reshape(1, d), ln2_b[i].reshape(1, d), t, tm)
    return h.reshape(b, t, d)
```
